```python
import jax, jax.numpy as jnp
from jax import lax
import numpy as np

D_MODEL = 2048
BATCH = 1
SEQ = 8192
DEPTH = 1
DEC_BATCH = 32
DEC_SEQ = 4
PAST_LEN = 16384
PAGE_SIZE = 128

A_GROUPS = 8
A_GROUP_DIM = 128
D_A = A_GROUPS * A_GROUP_DIM
CHUNK = 128
N_HEADS = 8
HEAD_DIM = 128
D_B = N_HEADS * HEAD_DIM
MOBA_BLOCK = 256
MOBA_TOPK = 3
Q_BLOCK = 128
D_IN = 3 * D_A + 4 * D_B + 2 * D_MODEL
POOL_NUM = 5
POOL_DEN = 4
EPS = 1e-6
NEG = -1e30

kernel_name = "gmlp_moba_gated_hybrid_step"


def rms_norm(x, w):
    xf = x.astype(jnp.float32)
    y = xf * lax.rsqrt(jnp.mean(xf * xf, axis=-1, keepdims=True) + EPS)
    return (y * w.astype(jnp.float32)).astype(x.dtype)


def alibi_slopes():
    return jnp.asarray(np.array([2.0 ** (-8.0 * (h + 1) / N_HEADS) for h in range(N_HEADS)], np.float32))


def project_in(x, norm_w, w_in, v_norm_w, q_norm_w, k_norm_w):
    lead = x.shape[:2]
    h = rms_norm(x, norm_w)
    proj = jnp.einsum('bsd,de->bse', h, w_in)
    sizes = (D_A, D_A, D_A, D_B, D_B, D_B, D_B, D_MODEL, D_MODEL)
    points = np.cumsum(sizes)[:-1].tolist()
    u, v, z_a, q, k, vb, z_b, g_a, g_b = jnp.split(proj, points, axis=-1)
    v = rms_norm(v.reshape(lead + (A_GROUPS, A_GROUP_DIM)), v_norm_w.reshape(A_GROUPS, A_GROUP_DIM))
    q = rms_norm(q.reshape(lead + (N_HEADS, HEAD_DIM)), q_norm_w)
    k = rms_norm(k.reshape(lead + (N_HEADS, HEAD_DIM)), k_norm_w)
    vb = vb.reshape(lead + (N_HEADS, HEAD_DIM))
    return u, v, z_a, q, k, vb, z_b, g_a, g_b


def chunk_spatial_gate(v, w_spatial, b_spatial):
    bt, n_pos = v.shape[:2]
    n_rows = min(n_pos, CHUNK)
    vc = v.reshape(bt, n_pos // n_rows, n_rows, A_GROUPS, A_GROUP_DIM)
    mask = jnp.tril(jnp.ones((n_rows, n_rows), dtype=bool))
    w = jnp.where(mask, w_spatial[:, :n_rows, :n_rows], 0)
    mixed = jnp.einsum('gts,bcsgd->bctgd', w, vc) + b_spatial[:, :n_rows].T[None, None, :, :, None]
    return mixed.reshape(bt, n_pos, D_A)


def moba_prompt(q, k, v):
    bsz, n_pos = q.shape[:2]
    nb = -(-n_pos // MOBA_BLOCK)
    s_pad = nb * MOBA_BLOCK
    pad = ((0, 0), (0, s_pad - n_pos), (0, 0), (0, 0))
    k_p = jnp.pad(k, pad)
    v_p = jnp.pad(v, pad)
    k_blocks = k_p.reshape(bsz, nb, MOBA_BLOCK, N_HEADS, HEAD_DIM)
    k_mean = jnp.mean(k_blocks.astype(jnp.float32), axis=2).astype(k.dtype)
    kb_t = k_blocks.transpose(0, 3, 1, 2, 4)
    vb_t = v_p.reshape(bsz, nb, MOBA_BLOCK, N_HEADS, HEAD_DIM).transpose(0, 3, 1, 2, 4)
    n_sel = min(MOBA_TOPK, nb)
    slopes = alibi_slopes()
    scale = HEAD_DIM ** -0.5
    b_idx = jnp.arange(bsz)[:, None, None, None]
    h_idx = jnp.arange(N_HEADS)[None, :, None, None]
    offs = jnp.arange(MOBA_BLOCK)

    def one_block(qb):
        start = qb * Q_BLOCK
        q_blk = lax.dynamic_slice_in_dim(q, start, Q_BLOCK, axis=1)
        t = start + jnp.arange(Q_BLOCK)
        own = start // MOBA_BLOCK
        gate = jnp.einsum('bqhd,bnhd->bhqn', q_blk, k_mean).astype(jnp.float32)
        gate = jnp.where(jnp.arange(nb) < own, gate, NEG)
        _, idx = lax.top_k(gate, n_sel)
        valid = jnp.arange(n_sel) < own
        k_sel = kb_t[b_idx, h_idx, idx]
        v_sel = vb_t[b_idx, h_idx, idx]
        pos_sel = idx[..., None] * MOBA_BLOCK + offs
        dist_sel = (t[:, None, None] - pos_sel).astype(jnp.float32)
        s_sel = jnp.einsum('bqhd,bhqjsd->bhqjs', q_blk, k_sel).astype(jnp.float32) * scale
        s_sel = s_sel - slopes[:, None, None, None] * dist_sel
        s_sel = jnp.where(valid[:, None], s_sel, NEG)
        own_start = own * MOBA_BLOCK
        k_own = lax.dynamic_slice_in_dim(k_p, own_start, MOBA_BLOCK, axis=1)
        v_own = lax.dynamic_slice_in_dim(v_p, own_start, MOBA_BLOCK, axis=1)
        dist_own = t[:, None] - (own_start + offs)[None, :]
        s_own = jnp.einsum('bqhd,bshd->bhqs', q_blk, k_own).astype(jnp.float32) * scale
        s_own = s_own - slopes[:, None, None] * dist_own.astype(jnp.float32)
        s_own = jnp.where(dist_own >= 0, s_own, NEG)
        n_flat = n_sel * MOBA_BLOCK
        scores = jnp.concatenate([s_sel.reshape(bsz, N_HEADS, Q_BLOCK, n_flat), s_own], axis=-1)
        p = jax.nn.softmax(scores, axis=-1).astype(v.dtype)
        p_sel = p[..., :n_flat].reshape(bsz, N_HEADS, Q_BLOCK, n_sel, MOBA_BLOCK)
        p_own = p[..., n_flat:]
        return (jnp.einsum('bhqjs,bhqjsd->bqhd', p_sel, v_sel)
                + jnp.einsum('bhqs,bshd->bqhd', p_own, v_own))

    out = lax.map(one_block, jnp.arange(n_pos // Q_BLOCK))
    return out.transpose(1, 0, 2, 3, 4).reshape(bsz, n_pos, N_HEADS, HEAD_DIM)


def moba_sample(q, k, v, cache_k, cache_v, page_table):
    n_bat, n_new = q.shape[:2]
    n_pages = page_table.shape[1]
    past = n_pages * PAGE_SIZE
    n_full = past // MOBA_BLOCK
    own_start = n_full * MOBA_BLOCK
    own_page = own_start // PAGE_SIZE
    r_own = past - own_start
    n_sel = min(MOBA_TOPK, n_full)
    slopes = alibi_slopes()
    scale = HEAD_DIM ** -0.5
    t = past + jnp.arange(n_new)
    own_tab = page_table[:, own_page:]
    k_own = jnp.concatenate([cache_k[own_tab].reshape(n_bat, r_own, N_HEADS, HEAD_DIM), k], axis=1)
    v_own = jnp.concatenate([cache_v[own_tab].reshape(n_bat, r_own, N_HEADS, HEAD_DIM), v], axis=1)
    dist_own = t[:, None] - (own_start + jnp.arange(r_own + n_new))[None, :]
    s_own = jnp.einsum('bqhd,bshd->bhqs', q, k_own).astype(jnp.float32) * scale
    s_own = s_own - slopes[:, None, None] * dist_own.astype(jnp.float32)
    s_own = jnp.where(dist_own >= 0, s_own, NEG)
    if n_sel == 0:
        p_own = jax.nn.softmax(s_own, axis=-1).astype(v.dtype)
        return jnp.einsum('bhqs,bshd->bqhd', p_own, v_own)
    k_past = cache_k[page_table[:, :own_page]].reshape(n_bat, n_full, MOBA_BLOCK, N_HEADS, HEAD_DIM)
    k_mean = jnp.mean(k_past.astype(jnp.float32), axis=2).astype(k.dtype)
    gate = jnp.einsum('bqhd,bnhd->bhqn', q, k_mean).astype(jnp.float32)
    _, idx = lax.top_k(gate, n_sel)
    ppb = MOBA_BLOCK // PAGE_SIZE
    logical = idx[..., None] * ppb + jnp.arange(ppb)
    phys = page_table[jnp.arange(n_bat)[:, None, None, None, None], logical]
    h_idx = jnp.arange(N_HEADS)[None, :, None, None, None, None]
    offs = jnp.arange(PAGE_SIZE)
    n_flat = n_sel * MOBA_BLOCK
    k_sel = cache_k[phys[..., None], offs, h_idx].reshape(n_bat, N_HEADS, n_new, n_flat, HEAD_DIM)
    v_sel = cache_v[phys[..., None], offs, h_idx].reshape(n_bat, N_HEADS, n_new, n_flat, HEAD_DIM)
    pos_sel = (idx[..., None] * MOBA_BLOCK + jnp.arange(MOBA_BLOCK)).reshape(n_bat, N_HEADS, n_new, n_flat)
    dist_sel = (t[:, None] - pos_sel).astype(jnp.float32)
    s_sel = jnp.einsum('bqhd,bhqsd->bhqs', q, k_sel).astype(jnp.float32) * scale
    s_sel = s_sel - slopes[:, None, None] * dist_sel
    scores = jnp.concatenate([s_sel, s_own], axis=-1)
    p = jax.nn.softmax(scores, axis=-1).astype(v.dtype)
    return (jnp.einsum('bhqs,bhqsd->bqhd', p[..., :n_flat], v_sel)
            + jnp.einsum('bhqs,bshd->bqhd', p[..., n_flat:], v_own))


def merge(x, u, mixed_a, z_a, o_b, z_b, g_a, g_b, w_out_a, w_out_b, w_out):
    lead = x.shape[:2]
    y_a = u * mixed_a * jax.nn.silu(z_a)
    y_b = o_b.reshape(lead + (D_B,)) * jax.nn.silu(z_b)
    br_a = jnp.einsum('bse,ed->bsd', y_a, w_out_a)
    br_b = jnp.einsum('bse,ed->bsd', y_b, w_out_b)
    m = jax.nn.sigmoid(g_a) * br_a + jax.nn.sigmoid(g_b) * br_b
    return x + jnp.einsum('bsd,de->bse', m, w_out)


def setup_inputs(seed: int = 0) -> dict:
    key = jax.random.key(seed)
    ks = jax.random.split(key, 16)
    f32 = jnp.float32
    n_pages = PAST_LEN // PAGE_SIZE
    n_used = DEC_BATCH * n_pages
    n_pool = (n_used * POOL_NUM) // POOL_DEN
    cache_shape = (DEPTH, n_pool, PAGE_SIZE, N_HEADS, HEAD_DIM)
    x_prompt = jax.random.normal(ks[0], (BATCH, SEQ, D_MODEL), f32)
    x_sample = jax.random.normal(ks[1], (DEC_BATCH, DEC_SEQ, D_MODEL), f32)
    cache_k = jax.random.normal(ks[2], cache_shape, f32)
    cache_v = jax.random.normal(ks[3], cache_shape, f32)
    page_table = jax.random.permutation(ks[4], n_pool)[:n_used].reshape(DEC_BATCH, n_pages).astype(jnp.int32)
    norm_w = 1.0 + 0.05 * jax.random.normal(ks[5], (DEPTH, D_MODEL), f32)
    w_in = jax.random.normal(ks[6], (DEPTH, D_MODEL, D_IN), f32) * D_MODEL ** -0.5
    v_norm_w = 1.0 + 0.05 * jax.random.normal(ks[7], (DEPTH, D_A), f32)
    q_norm_w = 1.0 + 0.05 * jax.random.normal(ks[8], (DEPTH, HEAD_DIM), f32)
    k_norm_w = 1.0 + 0.05 * jax.random.normal(ks[9], (DEPTH, HEAD_DIM), f32)
    w_spatial = jax.random.normal(ks[10], (DEPTH, A_GROUPS, CHUNK, CHUNK), f32) * CHUNK ** -0.5
    b_spatial = 1.0 + 0.05 * jax.random.normal(ks[11], (DEPTH, A_GROUPS, CHUNK), f32)
    w_out_a = jax.random.normal(ks[12], (DEPTH, D_A, D_MODEL), f32) * D_A ** -0.5
    w_out_b = jax.random.normal(ks[13], (DEPTH, D_B, D_MODEL), f32) * D_B ** -0.5
    w_out = jax.random.normal(ks[14], (DEPTH, D_MODEL, D_MODEL), f32) * D_MODEL ** -0.5
    return {"x_prompt": x_prompt, "x_sample": x_sample, "cache_k": cache_k, "cache_v": cache_v,
            "page_table": page_table, "norm_w": norm_w, "w_in": w_in, "v_norm_w": v_norm_w,
            "q_norm_w": q_norm_w, "k_norm_w": k_norm_w, "w_spatial": w_spatial, "b_spatial": b_spatial,
            "w_out_a": w_out_a, "w_out_b": w_out_b, "w_out": w_out}


def reference(x_prompt, x_sample, cache_k, cache_v, page_table, norm_w, w_in, v_norm_w, q_norm_w,
              k_norm_w, w_spatial, b_spatial, w_out_a, w_out_b, w_out):
    y_prompt, y_sample = x_prompt, x_sample
    kp_rows, vp_rows, ks_rows, vs_rows, cv_rows = [], [], [], [], []
    for layer in range(DEPTH):
        lw = (norm_w[layer], w_in[layer], v_norm_w[layer], q_norm_w[layer], k_norm_w[layer])
        outw = (w_out_a[layer], w_out_b[layer], w_out[layer])
        u, v, z_a, q, k, vb, z_b, g_a, g_b = project_in(y_prompt, *lw)
        mixed_a = chunk_spatial_gate(v, w_spatial[layer], b_spatial[layer])
        o_b = moba_prompt(q, k, vb)
        y_prompt = merge(y_prompt, u, mixed_a, z_a, o_b, z_b, g_a, g_b, *outw)
        kp_rows.append(k)
        vp_rows.append(vb)
        u, v, z_a, q, k, vb, z_b, g_a, g_b = project_in(y_sample, *lw)
        mixed_a = chunk_spatial_gate(v, w_spatial[layer], b_spatial[layer])
        o_b = moba_sample(q, k, vb, cache_k[layer], cache_v[layer], page_table)
        y_sample = merge(y_sample, u, mixed_a, z_a, o_b, z_b, g_a, g_b, *outw)
        ks_rows.append(k)
        vs_rows.append(vb)
        cv_rows.append(v.reshape(v.shape[0], v.shape[1], D_A))
    k_prompt = jnp.stack(kp_rows)
    v_prompt = jnp.stack(vp_rows)
    k_sample = jnp.stack(ks_rows)
    v_sample = jnp.stack(vs_rows)
    chunk_v_sample = jnp.stack(cv_rows)
    return (y_prompt, y_sample, k_prompt, v_prompt, k_sample, v_sample, chunk_v_sample)
```

```python
import functools

import jax
import jax.numpy as jnp
import numpy as np
from jax import lax
from jax.experimental import pallas as pl
from jax.experimental.pallas import tpu as pltpu

D_MODEL = 2048
A_GROUPS = 8
GROUP_DIM = 128
D_A = A_GROUPS * GROUP_DIM
CHUNK = 128
N_HEADS = 8
HEAD_DIM = 128
D_B = N_HEADS * HEAD_DIM
MOBA_BLOCK = 256
MOBA_TOPK = 3
PAGE_SIZE = 128
PAGES_PER_BLOCK = MOBA_BLOCK // PAGE_SIZE
SECTION = 1024
N_SECTIONS = 11
D_IN = N_SECTIONS * SECTION
EPS = 1e-6
NEG = -1e30
REMOVED = -3e38
SCALE = HEAD_DIM ** -0.5
VMEM_LIMIT = 56 * 1024 * 1024

_NT = (((1,), (1,)), ((), ()))
_BF16 = jnp.bfloat16
_F32 = jnp.float32


def _silu(z):
    return z * jax.nn.sigmoid(z)


def _proj_kernel(x_ref, nw_ref, w_ref, ew_ref,
                 u_ref, v_ref, za_ref, q_ref, k_ref, vb_ref, zb_ref, ga_ref, gb_ref,
                 h_ref, *, tn):
    j = pl.program_id(1)

    @pl.when(j == 0)
    def _():
        xf = x_ref[...]
        ms = jnp.mean(xf * xf, axis=-1, keepdims=True)
        h_ref[...] = (xf * lax.rsqrt(ms + EPS) * nw_ref[...]).astype(_BF16)

    acc = jnp.dot(h_ref[...], w_ref[...], preferred_element_type=_F32)
    tiles = SECTION // tn
    sec = j // tiles

    def group_normed():
        parts = []
        for c in range(tn // GROUP_DIM):
            blk = acc[:, c * GROUP_DIM:(c + 1) * GROUP_DIM]
            ms = jnp.mean(blk * blk, axis=-1, keepdims=True)
            parts.append(blk * lax.rsqrt(ms + EPS) * ew_ref[:, c * GROUP_DIM:(c + 1) * GROUP_DIM])
        return jnp.concatenate(parts, axis=-1)

    plain = ((0, u_ref), (2, za_ref), (5, vb_ref), (6, zb_ref))
    for s, ref in plain:
        @pl.when(sec == s)
        def _(ref=ref):
            ref[...] = acc.astype(ref.dtype)

    for s, ref in ((1, v_ref), (3, q_ref), (4, k_ref)):
        @pl.when(sec == s)
        def _(ref=ref):
            ref[...] = group_normed().astype(ref.dtype)

    @pl.when((sec == 7) | (sec == 8))
    def _():
        ga_ref[...] = acc.astype(ga_ref.dtype)

    @pl.when((sec == 9) | (sec == 10))
    def _():
        gb_ref[...] = acc.astype(gb_ref.dtype)


def _project(x2d, norm_w, w_bf16, epi_w, *, tm, tn, v_dtype):
    m = x2d.shape[0]
    tiles = SECTION // tn
    grid = (m // tm, D_IN // tn)

    def sec_spec(first_sec, n_sec):
        lo, n = first_sec * tiles, n_sec * tiles
        return pl.BlockSpec((tm, tn), lambda i, j: (i, jnp.clip(j - lo, 0, n - 1)))

    out_shapes = [
        jax.ShapeDtypeStruct((m, SECTION), _BF16),
        jax.ShapeDtypeStruct((m, SECTION), v_dtype),
        jax.ShapeDtypeStruct((m, SECTION), _BF16),
        jax.ShapeDtypeStruct((m, SECTION), _F32),
        jax.ShapeDtypeStruct((m, SECTION), _F32),
        jax.ShapeDtypeStruct((m, SECTION), _F32),
        jax.ShapeDtypeStruct((m, SECTION), _BF16),
        jax.ShapeDtypeStruct((m, 2 * SECTION), _BF16),
        jax.ShapeDtypeStruct((m, 2 * SECTION), _BF16),
    ]
    out_specs = [sec_spec(0, 1), sec_spec(1, 1), sec_spec(2, 1), sec_spec(3, 1), sec_spec(4, 1),
                 sec_spec(5, 1), sec_spec(6, 1), sec_spec(7, 2), sec_spec(9, 2)]
    return pl.pallas_call(
        functools.partial(_proj_kernel, tn=tn),
        grid=grid,
        in_specs=[
            pl.BlockSpec((tm, D_MODEL), lambda i, j: (i, 0)),
            pl.BlockSpec((1, D_MODEL), lambda i, j: (0, 0)),
            pl.BlockSpec((D_MODEL, tn), lambda i, j: (0, j)),
            pl.BlockSpec((1, tn), lambda i, j: (0, j)),
        ],
        out_specs=out_specs,
        out_shape=out_shapes,
        scratch_shapes=[pltpu.VMEM((tm, D_MODEL), _BF16)],
        compiler_params=pltpu.CompilerParams(
            dimension_semantics=("arbitrary", "arbitrary"), vmem_limit_bytes=VMEM_LIMIT),
        name="proj",
    )(x2d, norm_w, w_bf16, epi_w)


def _gmlp_kernel(u_ref, v_ref, za_ref, ws_ref, bs_ref, ya_ref, *, n_chunks):
    row = lax.broadcasted_iota(jnp.int32, (CHUNK, CHUNK), 0)
    col = lax.broadcasted_iota(jnp.int32, (CHUNK, CHUNK), 1)
    tril = row >= col
    for g in range(A_GROUPS):
        w_g = jnp.where(tril, ws_ref[g], 0.0).astype(_BF16)
        b_g = bs_ref[:, g:g + 1]
        cs = slice(g * GROUP_DIM, (g + 1) * GROUP_DIM)
        for c in range(n_chunks):
            rs = slice(c * CHUNK, (c + 1) * CHUNK)
            mixed = jnp.dot(w_g, v_ref[rs, cs], preferred_element_type=_F32) + b_g
            y = u_ref[rs, cs].astype(_F32) * mixed * _silu(za_ref[rs, cs].astype(_F32))
            ya_ref[rs, cs] = y.astype(ya_ref.dtype)


def _gmlp(u, v, z_a, w_spatial, b_spatial_t, *, tm):
    m = u.shape[0]
    row_spec = pl.BlockSpec((tm, D_A), lambda i: (i, 0))
    return pl.pallas_call(
        functools.partial(_gmlp_kernel, n_chunks=tm // CHUNK),
        grid=(m // tm,),
        in_specs=[row_spec, row_spec, row_spec,
                  pl.BlockSpec((A_GROUPS, CHUNK, CHUNK), lambda i: (0, 0, 0)),
                  pl.BlockSpec((CHUNK, A_GROUPS), lambda i: (0, 0))],
        out_specs=row_spec,
        out_shape=jax.ShapeDtypeStruct((m, D_A), _BF16),
        compiler_params=pltpu.CompilerParams(
            dimension_semantics=("arbitrary",), vmem_limit_bytes=VMEM_LIMIT),
        name="gmlp",
    )(u, v, z_a, w_spatial, b_spatial_t)


def _gmlp_dec_kernel(u_ref, v_ref, za_ref, wrow_ref, brow_ref, ya_ref, *, n_new):
    for t in range(n_new):
        mixed = brow_ref[t:t + 1, :]
        for s in range(t + 1):
            mixed = mixed + wrow_ref[t * n_new + s:t * n_new + s + 1, :] * v_ref[:, s, :]
        y = u_ref[:, t, :].astype(_F32) * mixed * _silu(za_ref[:, t, :].astype(_F32))
        ya_ref[:, t, :] = y.astype(ya_ref.dtype)


def _gmlp_dec(u, v, z_a, wrow, brow):
    n_bat, n_new, _ = v.shape
    full3 = pl.BlockSpec((n_bat, n_new, D_A), lambda i: (0, 0, 0))
    return pl.pallas_call(
        functools.partial(_gmlp_dec_kernel, n_new=n_new),
        grid=(1,),
        in_specs=[full3, full3, full3,
                  pl.BlockSpec(wrow.shape, lambda i: (0, 0)),
                  pl.BlockSpec(brow.shape, lambda i: (0, 0))],
        out_specs=full3,
        out_shape=jax.ShapeDtypeStruct((n_bat, n_new, D_A), _BF16),
        name="gmlp_dec",
    )(u, v, z_a, wrow, brow)


def _topk_mask(gate, col, n_valid, n_cols):
    g = jnp.where(col < n_valid, gate, NEG)
    sel = jnp.zeros(gate.shape, jnp.bool_)
    for j in range(MOBA_TOPK):
        m = jnp.max(g, axis=-1, keepdims=True)
        idx = jnp.min(jnp.where(g == m, col, n_cols), axis=-1, keepdims=True)
        pick = col == idx
        sel = sel | (pick & (j < n_valid))
        g = jnp.where(pick, REMOVED, g)
    return sel


def _moba_kernel(slopes_ref, q_ref, k_ref, v_ref, zb_ref, o_ref, kb_ref, vb_ref, km_ref, *, n_blocks):
    h = pl.program_id(0)
    qi = pl.program_id(1)
    blk = MOBA_BLOCK

    @pl.when(qi == 0)
    def _():
        def body(n, carry):
            rows = pl.ds(pl.multiple_of(n * blk, blk), blk)
            kf = k_ref[rows, :]
            kb_ref[rows, :] = kf.astype(_BF16)
            vb_ref[rows, :] = v_ref[rows, :].astype(_BF16)
            km_ref[pl.ds(n, 1), :] = jnp.mean(kf, axis=0, keepdims=True)
            return carry
        lax.fori_loop(0, n_blocks, body, 0)

    slope = slopes_ref[h]
    qf = q_ref[...]
    gate = lax.dot_general(qf, km_ref[...], _NT, precision=lax.Precision.HIGHEST,
                           preferred_element_type=_F32)
    gcol = lax.broadcasted_iota(jnp.int32, gate.shape, 1)
    sel = _topk_mask(gate, gcol, qi, n_blocks).astype(_F32)

    qb = qf.astype(_BF16)
    ri = lax.broadcasted_iota(jnp.int32, (blk, blk), 0)
    ci = lax.broadcasted_iota(jnp.int32, (blk, blk), 1)
    causal = ri >= ci
    d0 = (ri - ci).astype(_F32)

    own_rows = pl.ds(pl.multiple_of(qi * blk, blk), blk)
    s = lax.dot_general(qb, kb_ref[own_rows, :], _NT, preferred_element_type=_F32) * SCALE - slope * d0
    s = jnp.where(causal, s, NEG)
    m0 = jnp.max(s, axis=-1, keepdims=True)
    p = jnp.exp(s - m0)
    l0 = jnp.sum(p, axis=-1, keepdims=True)
    acc0 = jnp.dot(p.astype(_BF16), vb_ref[own_rows, :], preferred_element_type=_F32)

    def body(n, carry):
        m, l, acc = carry
        rows = pl.ds(pl.multiple_of(n * blk, blk), blk)
        dist = d0 + ((qi - n) * blk).astype(_F32)
        s = lax.dot_general(qb, kb_ref[rows, :], _NT, preferred_element_type=_F32) * SCALE - slope * dist
        picked = jnp.sum(jnp.where(gcol == n, sel, 0.0), axis=-1, keepdims=True) > 0.0
        s = jnp.where(picked, s, NEG)
        m_new = jnp.maximum(m, jnp.max(s, axis=-1, keepdims=True))
        alpha = jnp.exp(m - m_new)
        p = jnp.exp(s - m_new)
        l = alpha * l + jnp.sum(p, axis=-1, keepdims=True)
        acc = alpha * acc + jnp.dot(p.astype(_BF16), vb_ref[rows, :], preferred_element_type=_F32)
        return m_new, l, acc

    _, l, acc = lax.fori_loop(0, qi, body, (m0, l0, acc0))
    o = acc / l
    o_ref[...] = (o * _silu(zb_ref[...].astype(_F32))).astype(o_ref.dtype)


def _moba(q, k, v, z_b, slopes):
    n_pos = q.shape[0]
    n_blocks = n_pos // MOBA_BLOCK
    tile = pl.BlockSpec((MOBA_BLOCK, HEAD_DIM), lambda h, i: (i, h))
    head_all = pl.BlockSpec((n_pos, HEAD_DIM), lambda h, i: (0, h))
    return pl.pallas_call(
        functools.partial(_moba_kernel, n_blocks=n_blocks),
        grid=(N_HEADS, n_blocks),
        in_specs=[pl.BlockSpec(memory_space=pltpu.SMEM), tile, head_all, head_all, tile],
        out_specs=tile,
        out_shape=jax.ShapeDtypeStruct((n_pos, D_B), _BF16),
        scratch_shapes=[pltpu.VMEM((n_pos, HEAD_DIM), _BF16),
                        pltpu.VMEM((n_pos, HEAD_DIM), _BF16),
                        pltpu.VMEM((n_blocks, HEAD_DIM), _F32)],
        compiler_params=pltpu.CompilerParams(
            dimension_semantics=("arbitrary", "arbitrary"), vmem_limit_bytes=VMEM_LIMIT),
        name="moba",
    )(slopes, q, k, v, z_b)


def _kmean_kernel(pt_ref, q_ref, *rest, pages_per_step, n_full, n_new):
    page_refs = rest[:pages_per_step]
    idx_ref = rest[pages_per_step]
    km_ref = rest[pages_per_step + 1]
    s = pl.program_id(1)
    blocks_per_step = pages_per_step // PAGES_PER_BLOCK
    for j in range(blocks_per_step):
        tot = jnp.sum(page_refs[PAGES_PER_BLOCK * j][0], axis=0)
        for r in range(1, PAGES_PER_BLOCK):
            tot = tot + jnp.sum(page_refs[PAGES_PER_BLOCK * j + r][0], axis=0)
        km_ref[s * blocks_per_step + j] = tot * (1.0 / MOBA_BLOCK)

    @pl.when(s == pl.num_programs(1) - 1)
    def _():
        col = lax.broadcasted_iota(jnp.int32, (n_new, n_full), 1)
        lane = lax.broadcasted_iota(jnp.int32, (n_new, 128), 1)
        for h in range(N_HEADS):
            qh = q_ref[0, :, h * HEAD_DIM:(h + 1) * HEAD_DIM]
            gate = lax.dot_general(qh, km_ref[:, h, :], _NT, precision=lax.Precision.HIGHEST,
                                   preferred_element_type=_F32)
            g = gate
            tile = jnp.zeros((n_new, 128), jnp.int32)
            for j in range(MOBA_TOPK):
                m = jnp.max(g, axis=-1, keepdims=True)
                idx = jnp.min(jnp.where(g == m, col, n_full), axis=-1, keepdims=True)
                tile = jnp.where(lane == j, idx, tile)
                g = jnp.where(col == idx, REMOVED, g)
            idx_ref[0, h * n_new:(h + 1) * n_new, :] = tile


def _kmean_topk(page_table, q3, cache_k, *, pages_per_step):
    n_bat, n_pages = page_table.shape
    n_new = q3.shape[1]
    n_full = n_pages // PAGES_PER_BLOCK
    steps = n_pages // pages_per_step

    def page_spec(r):
        return pl.BlockSpec((1, PAGE_SIZE, N_HEADS, HEAD_DIM),
                            lambda b, s, pt: (pt[b, s * pages_per_step + r], 0, 0, 0))

    grid_spec = pltpu.PrefetchScalarGridSpec(
        num_scalar_prefetch=1,
        grid=(n_bat, steps),
        in_specs=[pl.BlockSpec((1, n_new, D_B), lambda b, s, pt: (b, 0, 0))]
        + [page_spec(r) for r in range(pages_per_step)],
        out_specs=pl.BlockSpec((1, N_HEADS * n_new, 128), lambda b, s, pt: (b, 0, 0)),
        scratch_shapes=[pltpu.VMEM((n_full, N_HEADS, HEAD_DIM), _F32)],
    )
    return pl.pallas_call(
        functools.partial(_kmean_kernel, pages_per_step=pages_per_step, n_full=n_full, n_new=n_new),
        grid_spec=grid_spec,
        out_shape=jax.ShapeDtypeStruct((n_bat, N_HEADS * n_new, 128), jnp.int32),
        compiler_params=pltpu.CompilerParams(
            dimension_semantics=("arbitrary", "arbitrary"), vmem_limit_bytes=VMEM_LIMIT),
        name="kmean",
    )(page_table, q3, *([cache_k] * pages_per_step))


def _moba_dec_kernel(pt_ref, idx_ref, slopes_ref, q_ref, kn_ref, vn_ref, zb_ref, ck_ref, cv_ref,
                     o_ref, kbuf, vbuf, sems, *, n_new, past):
    per_t = MOBA_TOPK * PAGES_PER_BLOCK
    n_flat = MOBA_TOPK * MOBA_BLOCK
    b = pl.program_id(0)
    h = pl.program_id(1)
    n_heads = pl.num_programs(1)
    step = b * n_heads + h
    n_steps = pl.num_programs(0) * n_heads
    slot = step % 2

    def page_copies(bb, hh, sl):
        copies = []
        for t in range(n_new):
            for c in range(per_t):
                blk_idx = idx_ref[bb, (hh * n_new + t) * MOBA_TOPK + c // PAGES_PER_BLOCK]
                page = pt_ref[bb, blk_idx * PAGES_PER_BLOCK + c % PAGES_PER_BLOCK]
                copies.append(pltpu.make_async_copy(
                    ck_ref.at[page, :, hh, :], kbuf.at[sl, t * per_t + c], sems.at[sl]))
                copies.append(pltpu.make_async_copy(
                    cv_ref.at[page, :, hh, :], vbuf.at[sl, t * per_t + c], sems.at[sl]))
        return copies

    @pl.when(step == 0)
    def _():
        for cp in page_copies(b, h, slot):
            cp.start()

    @pl.when(step + 1 < n_steps)
    def _():
        nxt = step + 1
        for cp in page_copies(nxt // n_heads, nxt % n_heads, 1 - slot):
            cp.start()

    for cp in page_copies(b, h, slot):
        cp.wait()

    slope = slopes_ref[h]
    qf = q_ref[0]
    qb = qf.astype(_BF16)
    row = lax.broadcasted_iota(jnp.int32, (n_new, n_flat), 0)
    lane = lax.broadcasted_iota(jnp.int32, (n_new, n_flat), 1)

    s_sel = jnp.zeros((n_new, n_flat), _F32)
    base = jnp.zeros((n_new, n_flat), jnp.int32)
    for t in range(n_new):
        k_cat = kbuf[slot, t * per_t:(t + 1) * per_t].reshape(n_flat, HEAD_DIM).astype(_BF16)
        s_t = lax.dot_general(qb, k_cat, _NT, preferred_element_type=_F32)
        s_sel = jnp.where(row == t, s_t, s_sel)
        for c in range(per_t):
            blk_idx = idx_ref[b, (h * n_new + t) * MOBA_TOPK + c // PAGES_PER_BLOCK]
            start = blk_idx * MOBA_BLOCK + (c % PAGES_PER_BLOCK) * PAGE_SIZE - c * PAGE_SIZE
            in_page = (row == t) & (lane >= c * PAGE_SIZE) & (lane < (c + 1) * PAGE_SIZE)
            base = jnp.where(in_page, start, base)
    dist_sel = (past + row - (base + lane)).astype(_F32)
    s_sel = s_sel * SCALE - slope * dist_sel

    lane_o = lax.broadcasted_iota(jnp.int32, (n_new, 128), 1)
    row_o = lax.broadcasted_iota(jnp.int32, (n_new, 128), 0)
    knf = kn_ref[0]
    s_own = jnp.full((n_new, 128), NEG, _F32)
    for t2 in range(n_new):
        dots = jnp.sum(qf * knf[t2:t2 + 1, :], axis=-1, keepdims=True)
        s_own = jnp.where(lane_o == t2, dots, s_own)
    dist_own = (row_o - lane_o).astype(_F32)
    s_own = jnp.where((lane_o < n_new) & (row_o >= lane_o), s_own * SCALE - slope * dist_own, NEG)

    m = jnp.maximum(jnp.max(s_sel, axis=-1, keepdims=True), jnp.max(s_own, axis=-1, keepdims=True))
    p_sel = jnp.exp(s_sel - m)
    p_own = jnp.exp(s_own - m)
    l = jnp.sum(p_sel, axis=-1, keepdims=True) + jnp.sum(p_own, axis=-1, keepdims=True)
    p_sel = (p_sel / l).astype(_BF16)
    p_own = p_own / l

    row_d = lax.broadcasted_iota(jnp.int32, (n_new, HEAD_DIM), 0)
    o = jnp.zeros((n_new, HEAD_DIM), _F32)
    for t in range(n_new):
        v_cat = vbuf[slot, t * per_t:(t + 1) * per_t].reshape(n_flat, HEAD_DIM).astype(_BF16)
        o_t = jnp.dot(p_sel, v_cat, preferred_element_type=_F32)
        o = jnp.where(row_d == t, o_t, o)
    vnf = vn_ref[0]
    for t2 in range(n_new):
        w = jnp.sum(jnp.where(lane_o == t2, p_own, 0.0), axis=-1, keepdims=True)
        o = o + w * vnf[t2:t2 + 1, :]
    o_ref[0] = (o * _silu(zb_ref[0].astype(_F32))).astype(o_ref.dtype)


def _moba_dec(page_table, idx, slopes, q3, k3, v3, zb3, cache_k, cache_v, *, past):
    n_bat, n_new, _ = q3.shape
    n_pg = n_new * MOBA_TOPK * PAGES_PER_BLOCK
    tok = pl.BlockSpec((1, n_new, HEAD_DIM), lambda b, h, pt, ix: (b, 0, h))
    hbm = pl.BlockSpec(memory_space=pl.ANY)
    grid_spec = pltpu.PrefetchScalarGridSpec(
        num_scalar_prefetch=2,
        grid=(n_bat, N_HEADS),
        in_specs=[pl.BlockSpec(memory_space=pltpu.SMEM), tok, tok, tok, tok, hbm, hbm],
        out_specs=tok,
        scratch_shapes=[pltpu.VMEM((2, n_pg, PAGE_SIZE, HEAD_DIM), _F32),
                        pltpu.VMEM((2, n_pg, PAGE_SIZE, HEAD_DIM), _F32),
                        pltpu.SemaphoreType.DMA((2,))],
    )
    return pl.pallas_call(
        functools.partial(_moba_dec_kernel, n_new=n_new, past=past),
        grid_spec=grid_spec,
        out_shape=jax.ShapeDtypeStruct((n_bat, n_new, D_B), _BF16),
        compiler_params=pltpu.CompilerParams(
            dimension_semantics=("arbitrary", "arbitrary"), vmem_limit_bytes=VMEM_LIMIT),
        name="moba_dec",
    )(page_table, idx, slopes, q3, k3, v3, zb3, cache_k, cache_v)


def _merge_kernel(x_ref, ya_ref, yb_ref, ga_ref, gb_ref, woa_ref, wob_ref, wo_ref, y_ref):
    br_a = jnp.dot(ya_ref[...], woa_ref[...], preferred_element_type=_F32)
    br_b = jnp.dot(yb_ref[...], wob_ref[...], preferred_element_type=_F32)
    mix = (jax.nn.sigmoid(ga_ref[...].astype(_F32)) * br_a
           + jax.nn.sigmoid(gb_ref[...].astype(_F32)) * br_b)
    y_ref[...] = x_ref[...] + jnp.dot(mix.astype(_BF16), wo_ref[...], preferred_element_type=_F32)


def _merge(x2d, y_a, y_b, g_a, g_b, w_oa, w_ob, w_o, *, tm):
    m = x2d.shape[0]

    def rows(width):
        return pl.BlockSpec((tm, width), lambda i: (i, 0))

    def whole(shape):
        return pl.BlockSpec(shape, lambda i: (0, 0), pipeline_mode=pl.Buffered(1))

    return pl.pallas_call(
        _merge_kernel,
        grid=(m // tm,),
        in_specs=[rows(D_MODEL), rows(D_A), rows(D_B), rows(D_MODEL), rows(D_MODEL),
                  whole(w_oa.shape), whole(w_ob.shape), whole(w_o.shape)],
        out_specs=rows(D_MODEL),
        out_shape=jax.ShapeDtypeStruct((m, D_MODEL), _F32),
        compiler_params=pltpu.CompilerParams(
            dimension_semantics=("arbitrary",), vmem_limit_bytes=VMEM_LIMIT),
        name="merge",
    )(x2d, y_a, y_b, g_a, g_b, w_oa, w_ob, w_o)


def kernel(x_prompt, x_sample, cache_k, cache_v, page_table, norm_w, w_in, v_norm_w, q_norm_w,
           k_norm_w, w_spatial, b_spatial, w_out_a, w_out_b, w_out):
    depth = norm_w.shape[0]
    assert depth == 1, "single-layer trunk"
    n_bp, seq, _ = x_prompt.shape
    assert n_bp == 1 and seq % MOBA_BLOCK == 0
    n_bat, n_new, _ = x_sample.shape
    n_pages = page_table.shape[1]
    past = n_pages * PAGE_SIZE
    assert past % MOBA_BLOCK == 0 and n_pages // PAGES_PER_BLOCK >= MOBA_TOPK and n_new <= CHUNK
    n_pool = cache_k.shape[1]
    layer = 0

    slopes = jnp.asarray(np.array([2.0 ** (-8.0 * (h + 1) / N_HEADS) for h in range(N_HEADS)], np.float32))
    ones = jnp.ones((SECTION,), _F32)
    epi_w = jnp.concatenate([
        ones, v_norm_w[layer], ones, jnp.tile(q_norm_w[layer], N_HEADS), jnp.tile(k_norm_w[layer], N_HEADS),
        ones, ones, ones, ones, ones, ones]).reshape(1, D_IN)
    nw = norm_w[layer].reshape(1, D_MODEL)
    w_in_b = w_in[layer].astype(_BF16)
    w_oa_b = w_out_a[layer].astype(_BF16)
    w_ob_b = w_out_b[layer].astype(_BF16)
    w_o_b = w_out[layer].astype(_BF16)

    xp = x_prompt.reshape(seq, D_MODEL)
    u, v, z_a, q, k, vb, z_b, g_a, g_b = _project(xp, nw, w_in_b, epi_w, tm=512, tn=512, v_dtype=_BF16)
    y_a = _gmlp(u, v, z_a, w_spatial[layer], b_spatial[layer].T, tm=512)
    y_b = _moba(q, k, vb, z_b, slopes)
    y_prompt = _merge(xp, y_a, y_b, g_a, g_b, w_oa_b, w_ob_b, w_o_b, tm=256)

    m_s = n_bat * n_new
    xs = x_sample.reshape(m_s, D_MODEL)
    u_s, v_s, za_s, q_s, k_s, vb_s, zb_s, ga_s, gb_s = _project(
        xs, nw, w_in_b, epi_w, tm=m_s, tn=512, v_dtype=_F32)
    w4 = w_spatial[layer][:, :n_new, :n_new]
    wrow = jnp.repeat(w4.transpose(1, 2, 0), GROUP_DIM, axis=-1).reshape(n_new * n_new, D_A)
    brow = jnp.repeat(b_spatial[layer][:, :n_new].T, GROUP_DIM, axis=-1)
    three = (n_bat, n_new, D_A)
    ya_s = _gmlp_dec(u_s.reshape(three), v_s.reshape(three), za_s.reshape(three), wrow, brow)
    q3 = q_s.reshape(n_bat, n_new, D_B)
    idx_pad = _kmean_topk(page_table, q3, cache_k[layer], pages_per_step=16)
    idx = idx_pad[:, :, :MOBA_TOPK].reshape(n_bat, N_HEADS * n_new * MOBA_TOPK)
    yb_s = _moba_dec(page_table, idx, slopes, q3, k_s.reshape(n_bat, n_new, D_B),
                     vb_s.reshape(n_bat, n_new, D_B), zb_s.reshape(n_bat, n_new, D_B),
                     cache_k[layer], cache_v[layer], past=past)
    y_sample = _merge(xs, ya_s.reshape(m_s, D_A), yb_s.reshape(m_s, D_B), ga_s, gb_s,
                      w_oa_b, w_ob_b, w_o_b, tm=m_s)

    return (y_prompt.reshape(x_prompt.shape),
            y_sample.reshape(x_sample.shape),
            k.reshape(depth, n_bp, seq, N_HEADS, HEAD_DIM),
            vb.reshape(depth, n_bp, seq, N_HEADS, HEAD_DIM),
            k_s.reshape(depth, n_bat, n_new, N_HEADS, HEAD_DIM),
            vb_s.reshape(depth, n_bat, n_new, N_HEADS, HEAD_DIM),
            v_s.reshape(depth, n_bat, n_new, D_A))
```

```python
import functools

import jax
import jax.numpy as jnp
import numpy as np
from jax import lax
from jax.experimental import pallas as pl
from jax.experimental.pallas import tpu as pltpu

D_MODEL = 2048
A_GROUPS = 8
GROUP_DIM = 128
D_A = A_GROUPS * GROUP_DIM
CHUNK = 128
N_HEADS = 8
HEAD_DIM = 128
D_B = N_HEADS * HEAD_DIM
MOBA_BLOCK = 256
MOBA_TOPK = 3
PAGE_SIZE = 128
PAGES_PER_BLOCK = MOBA_BLOCK // PAGE_SIZE
SECTION = 1024
N_SECTIONS = 11
D_IN = N_SECTIONS * SECTION
EPS = 1e-6
NEG = -1e30
REMOVED = -3e38
SCALE = HEAD_DIM ** -0.5
VMEM_LIMIT = 56 * 1024 * 1024

_NT = (((1,), (1,)), ((), ()))
_BF16 = jnp.bfloat16
_F32 = jnp.float32


def _silu(z):
    return z * jax.nn.sigmoid(z)


W_SECTIONS_F32 = (1, 3, 4, 5)
NORMED_F32 = (1, 1, 1, 0)
W_SECTIONS_BF16 = (7, 8, 9, 10, 0, 2, 6)
F_V, F_Q, F_K, F_VB = 0, 1, 2, 3
P_GA, P_GB, P_U, P_ZA, P_ZB = 0, 2, 4, 5, 6


def _proj_kernel(wcol_ref, normed_ref, x_ref, nw_ref, w_ref, ew_ref, o_ref, h_ref, *, tn, grouped):
    j = pl.program_id(1)

    @pl.when(j == 0)
    def _():
        xf = x_ref[...]
        ms = jnp.mean(xf * xf, axis=-1, keepdims=True)
        h_ref[...] = (xf * lax.rsqrt(ms + EPS) * nw_ref[...]).astype(_BF16)

    acc = jnp.dot(h_ref[...], w_ref[...], preferred_element_type=_F32)
    if not grouped:
        o_ref[...] = acc.astype(o_ref.dtype)
        return
    use_norm = normed_ref[j] != 0
    for c in range(tn // GROUP_DIM):
        cs = slice(c * GROUP_DIM, (c + 1) * GROUP_DIM)
        blk = acc[:, cs]
        ms = jnp.mean(blk * blk, axis=-1, keepdims=True)
        scale = jnp.where(use_norm, lax.rsqrt(ms + EPS), 1.0)
        o_ref[:, cs] = (blk * scale * ew_ref[:, cs]).astype(o_ref.dtype)


def _project(x2d, norm_w, w_bf16, epi_w, *, sections, normed, out_dtype, tm, tn):
    m = x2d.shape[0]
    tiles = SECTION // tn
    n_tiles = len(sections) * tiles
    wcol = np.array([s * tiles + t for s in sections for t in range(tiles)], np.int32)
    flags = np.array([f for f in (normed or (0,) * len(sections)) for _ in range(tiles)], np.int32)
    grid_spec = pltpu.PrefetchScalarGridSpec(
        num_scalar_prefetch=2,
        grid=(m // tm, n_tiles),
        in_specs=[
            pl.BlockSpec((tm, D_MODEL), lambda i, j, wc, fl: (i, 0)),
            pl.BlockSpec((1, D_MODEL), lambda i, j, wc, fl: (0, 0)),
            pl.BlockSpec((D_MODEL, tn), lambda i, j, wc, fl: (0, wc[j])),
            pl.BlockSpec((1, tn), lambda i, j, wc, fl: (0, wc[j])),
        ],
        out_specs=pl.BlockSpec((tm, tn), lambda i, j, wc, fl: (i, j)),
        scratch_shapes=[pltpu.VMEM((tm, D_MODEL), _BF16)],
    )
    return pl.pallas_call(
        functools.partial(_proj_kernel, tn=tn, grouped=normed is not None),
        grid_spec=grid_spec,
        out_shape=jax.ShapeDtypeStruct((m, len(sections) * SECTION), out_dtype),
        compiler_params=pltpu.CompilerParams(
            dimension_semantics=("arbitrary", "arbitrary"), vmem_limit_bytes=VMEM_LIMIT),
        name="proj_f32" if normed is not None else "proj_bf16",
    )(jnp.asarray(wcol), jnp.asarray(flags), x2d, norm_w, w_bf16, epi_w)


def _project_both(x2d, norm_w, w_bf16, epi_w, *, tm, tn):
    pf = _project(x2d, norm_w, w_bf16, epi_w, sections=W_SECTIONS_F32, normed=NORMED_F32,
                  out_dtype=_F32, tm=tm, tn=tn)
    pb = _project(x2d, norm_w, w_bf16, epi_w, sections=W_SECTIONS_BF16, normed=None,
                  out_dtype=_BF16, tm=tm, tn=tn)
    return pf, pb


def _gmlp_kernel(u_ref, v_ref, za_ref, ws_ref, bs_ref, ya_ref, *, n_chunks):
    row = lax.broadcasted_iota(jnp.int32, (CHUNK, CHUNK), 0)
    col = lax.broadcasted_iota(jnp.int32, (CHUNK, CHUNK), 1)
    tril = row >= col
    for g in range(A_GROUPS):
        w_g = jnp.where(tril, ws_ref[g], 0.0).astype(_BF16)
        b_g = bs_ref[:, g:g + 1]
        cs = slice(g * GROUP_DIM, (g + 1) * GROUP_DIM)
        for c in range(n_chunks):
            rs = slice(c * CHUNK, (c + 1) * CHUNK)
            mixed = jnp.dot(w_g, v_ref[rs, cs].astype(_BF16), preferred_element_type=_F32) + b_g
            y = u_ref[rs, cs].astype(_F32) * mixed * _silu(za_ref[rs, cs].astype(_F32))
            ya_ref[rs, cs] = y.astype(ya_ref.dtype)


def _gmlp(pf, pb, w_spatial, b_spatial_t, *, tm):
    m = pf.shape[0]

    def section(c):
        return pl.BlockSpec((tm, D_A), lambda i: (i, c))

    row_spec = section(0)
    return pl.pallas_call(
        functools.partial(_gmlp_kernel, n_chunks=tm // CHUNK),
        grid=(m // tm,),
        in_specs=[section(P_U), section(F_V), section(P_ZA),
                  pl.BlockSpec((A_GROUPS, CHUNK, CHUNK), lambda i: (0, 0, 0)),
                  pl.BlockSpec((CHUNK, A_GROUPS), lambda i: (0, 0))],
        out_specs=row_spec,
        out_shape=jax.ShapeDtypeStruct((m, D_A), _BF16),
        compiler_params=pltpu.CompilerParams(
            dimension_semantics=("arbitrary",), vmem_limit_bytes=VMEM_LIMIT),
        name="gmlp",
    )(pb, pf, pb, w_spatial, b_spatial_t)


def _gmlp_dec_kernel(u_ref, v_ref, za_ref, wrow_ref, brow_ref, ya_ref, *, n_new):
    for t in range(n_new):
        mixed = brow_ref[t:t + 1, :]
        for s in range(t + 1):
            mixed = mixed + wrow_ref[t * n_new + s:t * n_new + s + 1, :] * v_ref[:, s, :]
        y = u_ref[:, t, :].astype(_F32) * mixed * _silu(za_ref[:, t, :].astype(_F32))
        ya_ref[:, t, :] = y.astype(ya_ref.dtype)


def _gmlp_dec(u, v, z_a, wrow, brow):
    n_bat, n_new, _ = v.shape
    full3 = pl.BlockSpec((n_bat, n_new, D_A), lambda i: (0, 0, 0))
    return pl.pallas_call(
        functools.partial(_gmlp_dec_kernel, n_new=n_new),
        grid=(1,),
        in_specs=[full3, full3, full3,
                  pl.BlockSpec(wrow.shape, lambda i: (0, 0)),
                  pl.BlockSpec(brow.shape, lambda i: (0, 0))],
        out_specs=full3,
        out_shape=jax.ShapeDtypeStruct((n_bat, n_new, D_A), _BF16),
        name="gmlp_dec",
    )(u, v, z_a, wrow, brow)


def _topk_mask(gate, col, n_valid, n_cols):
    g = jnp.where(col < n_valid, gate, NEG)
    sel = jnp.zeros(gate.shape, jnp.bool_)
    for j in range(MOBA_TOPK):
        m = jnp.max(g, axis=-1, keepdims=True)
        idx = jnp.min(jnp.where(g == m, col, n_cols), axis=-1, keepdims=True)
        pick = col == idx
        sel = sel | (pick & (j < n_valid))
        g = jnp.where(pick, REMOVED, g)
    return sel


def _bf16_pieces(x, n):
    out = []
    for _ in range(n):
        piece = float(np.asarray(x, np.float32).astype(_BF16).astype(np.float32))
        out.append(piece)
        x = x - piece
    return tuple(out)


_LOG2E = 1.4426950408889634
_LOG2E_PIECES = _bf16_pieces(_LOG2E, 3)
_N_PIECES = len(_LOG2E_PIECES)


def _lane_values(lane, first, values):
    out = jnp.zeros(lane.shape, _F32)
    for p, val in enumerate(values):
        out = jnp.where(lane == first + p, val, out)
    return out


def _moba_kernel(slopes_ref, q_ref, k_ref, v_ref, zb_ref, o_ref,
                 kaug_ref, vb_ref, km_ref, qa_ref, m_ref, l_ref, acc_ref, sa_ref, sb_ref, *, n_blocks, sub):
    h = pl.program_id(0)
    qi = pl.program_id(1)
    blk = MOBA_BLOCK
    lane_hi = n_blocks
    lane_lo = lane_hi + _N_PIECES
    lane_t0 = lane_lo + _N_PIECES
    slope = slopes_ref[h]
    coef = [slope * c for c in _LOG2E_PIECES]
    lane = lax.broadcasted_iota(jnp.int32, (blk, HEAD_DIM), 1)
    hi_lanes = (lane >= lane_hi) & (lane < lane_lo)

    @pl.when(qi == 0)
    def _():
        km_ref[...] = jnp.zeros(km_ref.shape, _F32)
        row_f = lax.broadcasted_iota(jnp.int32, (blk, HEAD_DIM), 0).astype(_F32)
        fixed = (_lane_values(lane, lane_t0, [-c * blk for c in coef])
                 + jnp.where((lane >= lane_lo) & (lane < lane_t0), row_f, 0.0))

        def body(n, carry):
            rows = pl.ds(pl.multiple_of(n * blk, blk), blk)
            kf = k_ref[rows, :]
            aug = fixed + jnp.where(lane == n, NEG, 0.0) + jnp.where(hi_lanes, jnp.asarray(n * blk, _F32), 0.0)
            kaug_ref[rows, :HEAD_DIM] = kf.astype(_BF16)
            kaug_ref[rows, HEAD_DIM:] = aug.astype(_BF16)
            vb_ref[rows, :] = v_ref[rows, :].astype(_BF16)
            km_ref[pl.ds(n, 1), :] = jnp.mean(kf, axis=0, keepdims=True)
            return carry
        lax.fori_loop(0, n_blocks, body, 0)

    q_consts = (_lane_values(lane, lane_hi, coef) + _lane_values(lane, lane_lo, coef)
                + jnp.where((lane >= lane_t0) & (lane < lane_t0 + _N_PIECES), jnp.asarray(qi * sub, _F32), 0.0))
    for r in range(sub):
        rs = slice(r * blk, (r + 1) * blk)
        qf = q_ref[rs, :]
        gate = lax.dot_general(qf, km_ref[...], _NT, precision=lax.Precision.HIGHEST,
                               preferred_element_type=_F32)
        own = qi * sub + r
        open_blocks = _topk_mask(gate, lane, own, HEAD_DIM) | (lane == own)
        qa_ref[rs, :HEAD_DIM] = (qf * (SCALE * _LOG2E)).astype(_BF16)
        qa_ref[rs, HEAD_DIM:] = jnp.where(lane < n_blocks, jnp.where(open_blocks, 0.0, 1.0), q_consts).astype(_BF16)

    def scores(r, k_tile):
        rs = slice(r * blk, (r + 1) * blk)
        return lax.dot_general(qa_ref[rs, :], k_tile, _NT, preferred_element_type=_F32)

    def attend(r, s, v_tile, first):
        rs = slice(r * blk, (r + 1) * blk)
        s0, s1 = s[:, :HEAD_DIM], s[:, HEAD_DIM:]
        mx = jnp.max(jnp.maximum(s0, s1), axis=-1, keepdims=True)
        if first:
            m_new = jnp.broadcast_to(mx, (blk, HEAD_DIM))
        else:
            m_old = m_ref[rs, :]
            m_new = jnp.maximum(m_old, mx)
        p0 = jnp.exp2(s0 - m_new)
        p1 = jnp.exp2(s1 - m_new)
        row_sum = jnp.sum(p0 + p1, axis=-1, keepdims=True)
        pv = jnp.dot(jnp.concatenate([p0, p1], axis=-1).astype(_BF16), v_tile, preferred_element_type=_F32)
        if first:
            l_ref[rs, :] = jnp.broadcast_to(row_sum, (blk, HEAD_DIM))
            acc_ref[rs, :] = pv
        else:
            alpha = jnp.exp2(m_old - m_new)
            l_ref[rs, :] = alpha * l_ref[rs, :] + row_sum
            acc_ref[rs, :] = alpha * acc_ref[rs, :] + pv
        m_ref[rs, :] = m_new

    ri = lax.broadcasted_iota(jnp.int32, (blk, blk), 0)
    ci = lax.broadcasted_iota(jnp.int32, (blk, blk), 1)
    causal = ri >= ci
    for r in range(sub):
        for r2 in [r] + list(range(r)):
            rows = pl.ds(pl.multiple_of((qi * sub + r2) * blk, blk), blk)
            s = scores(r, kaug_ref[rows, :])
            attend(r, jnp.where(causal, s, NEG) if r2 == r else s, vb_ref[rows, :], first=(r2 == r))

    n_past = qi * sub

    def block_rows(n):
        return pl.ds(pl.multiple_of(n * blk, blk), blk)

    def produce(n, dst_ref):
        k_tile = kaug_ref[block_rows(n), :]
        for r in range(sub):
            dst_ref[r * blk:(r + 1) * blk, :] = scores(r, k_tile)

    def consume(n, src_ref):
        v_tile = vb_ref[block_rows(n), :]
        for r in range(sub):
            attend(r, src_ref[r * blk:(r + 1) * blk, :], v_tile, first=False)

    @pl.when(n_past > 0)
    def _():
        produce(0, sa_ref)

    def body(j, carry):
        n0 = 2 * j
        produce(n0 + 1, sb_ref)
        consume(n0, sa_ref)
        produce(jnp.minimum(n0 + 2, n_past - 1), sa_ref)
        consume(n0 + 1, sb_ref)
        return carry
    lax.fori_loop(0, n_past // 2, body, 0)

    o = acc_ref[...] / l_ref[...]
    o_ref[...] = (o * _silu(zb_ref[...].astype(_F32))).astype(o_ref.dtype)


def _moba(pf, pb, slopes, *, sub):
    n_pos = pf.shape[0]
    n_blocks = n_pos // MOBA_BLOCK
    tq = sub * MOBA_BLOCK
    assert n_pos % tq == 0 and sub % 2 == 0 and n_blocks + 3 * _N_PIECES <= HEAD_DIM

    def tile_of(section):
        return pl.BlockSpec((tq, HEAD_DIM), lambda h, i: (i, section * N_HEADS + h))

    def head_all(section):
        return pl.BlockSpec((n_pos, HEAD_DIM), lambda h, i: (0, section * N_HEADS + h))

    return pl.pallas_call(
        functools.partial(_moba_kernel, n_blocks=n_blocks, sub=sub),
        grid=(N_HEADS, n_pos // tq),
        in_specs=[pl.BlockSpec(memory_space=pltpu.SMEM), tile_of(F_Q), head_all(F_K), head_all(F_VB),
                  tile_of(P_ZB)],
        out_specs=tile_of(0),
        out_shape=jax.ShapeDtypeStruct((n_pos, D_B), _BF16),
        scratch_shapes=[pltpu.VMEM((n_pos, 2 * HEAD_DIM), _BF16),
                        pltpu.VMEM((n_pos, HEAD_DIM), _BF16),
                        pltpu.VMEM((HEAD_DIM, HEAD_DIM), _F32),
                        pltpu.VMEM((tq, 2 * HEAD_DIM), _BF16),
                        pltpu.VMEM((tq, HEAD_DIM), _F32),
                        pltpu.VMEM((tq, HEAD_DIM), _F32),
                        pltpu.VMEM((tq, HEAD_DIM), _F32),
                        pltpu.VMEM((tq, MOBA_BLOCK), _F32),
                        pltpu.VMEM((tq, MOBA_BLOCK), _F32)],
        compiler_params=pltpu.CompilerParams(
            dimension_semantics=("arbitrary", "arbitrary"), vmem_limit_bytes=VMEM_LIMIT),
        name="moba",
    )(slopes, pf, pf, pf, pb)


def _kmean_kernel(pt_ref, q_ref, *rest, pages_per_step, n_full, n_new):
    page_refs = rest[:pages_per_step]
    idx_ref = rest[pages_per_step]
    km_ref = rest[pages_per_step + 1]
    s = pl.program_id(1)
    blocks_per_step = pages_per_step // PAGES_PER_BLOCK
    for j in range(blocks_per_step):
        tot = jnp.sum(page_refs[PAGES_PER_BLOCK * j][0], axis=0)
        for r in range(1, PAGES_PER_BLOCK):
            tot = tot + jnp.sum(page_refs[PAGES_PER_BLOCK * j + r][0], axis=0)
        km_ref[s * blocks_per_step + j] = tot * (1.0 / MOBA_BLOCK)

    @pl.when(s == pl.num_programs(1) - 1)
    def _():
        col = lax.broadcasted_iota(jnp.int32, (n_new, n_full), 1)
        lane = lax.broadcasted_iota(jnp.int32, (n_new, 128), 1)
        for h in range(N_HEADS):
            qh = q_ref[0, :, h * HEAD_DIM:(h + 1) * HEAD_DIM]
            gate = lax.dot_general(qh, km_ref[:, h, :], _NT, precision=lax.Precision.HIGHEST,
                                   preferred_element_type=_F32)
            g = gate
            tile = jnp.zeros((n_new, 128), jnp.int32)
            for j in range(MOBA_TOPK):
                m = jnp.max(g, axis=-1, keepdims=True)
                idx = jnp.min(jnp.where(g == m, col, n_full), axis=-1, keepdims=True)
                tile = jnp.where(lane == j, idx, tile)
                g = jnp.where(col == idx, REMOVED, g)
            idx_ref[0, h * n_new:(h + 1) * n_new, :] = tile


def _kmean_topk(page_table, q3, cache_k, *, pages_per_step):
    n_bat, n_pages = page_table.shape
    n_new = q3.shape[1]
    n_full = n_pages // PAGES_PER_BLOCK
    steps = n_pages // pages_per_step

    def page_spec(r):
        return pl.BlockSpec((1, PAGE_SIZE, N_HEADS, HEAD_DIM),
                            lambda b, s, pt: (pt[b, s * pages_per_step + r], 0, 0, 0))

    grid_spec = pltpu.PrefetchScalarGridSpec(
        num_scalar_prefetch=1,
        grid=(n_bat, steps),
        in_specs=[pl.BlockSpec((1, n_new, D_B), lambda b, s, pt: (b, 0, 0))]
        + [page_spec(r) for r in range(pages_per_step)],
        out_specs=pl.BlockSpec((1, N_HEADS * n_new, 128), lambda b, s, pt: (b, 0, 0)),
        scratch_shapes=[pltpu.VMEM((n_full, N_HEADS, HEAD_DIM), _F32)],
    )
    return pl.pallas_call(
        functools.partial(_kmean_kernel, pages_per_step=pages_per_step, n_full=n_full, n_new=n_new),
        grid_spec=grid_spec,
        out_shape=jax.ShapeDtypeStruct((n_bat, N_HEADS * n_new, 128), jnp.int32),
        compiler_params=pltpu.CompilerParams(
            dimension_semantics=("arbitrary", "arbitrary"), vmem_limit_bytes=VMEM_LIMIT),
        name="kmean",
    )(page_table, q3, *([cache_k] * pages_per_step))


def _moba_dec_kernel(pt_ref, idx_ref, slopes_ref, q_ref, kn_ref, vn_ref, zb_ref, ck_ref, cv_ref,
                     o_ref, kbuf, vbuf, sems, *, n_new, past):
    per_t = MOBA_TOPK * PAGES_PER_BLOCK
    n_flat = MOBA_TOPK * MOBA_BLOCK
    b = pl.program_id(0)
    h = pl.program_id(1)
    n_heads = pl.num_programs(1)
    step = b * n_heads + h
    n_steps = pl.num_programs(0) * n_heads
    slot = step % 2

    def page_copies(bb, hh, sl):
        copies = []
        for t in range(n_new):
            for c in range(per_t):
                blk_idx = idx_ref[bb, (hh * n_new + t) * MOBA_TOPK + c // PAGES_PER_BLOCK]
                page = pt_ref[bb, blk_idx * PAGES_PER_BLOCK + c % PAGES_PER_BLOCK]
                copies.append(pltpu.make_async_copy(
                    ck_ref.at[page, :, hh, :], kbuf.at[sl, t * per_t + c], sems.at[sl]))
                copies.append(pltpu.make_async_copy(
                    cv_ref.at[page, :, hh, :], vbuf.at[sl, t * per_t + c], sems.at[sl]))
        return copies

    @pl.when(step == 0)
    def _():
        for cp in page_copies(b, h, slot):
            cp.start()

    @pl.when(step + 1 < n_steps)
    def _():
        nxt = step + 1
        for cp in page_copies(nxt // n_heads, nxt % n_heads, 1 - slot):
            cp.start()

    for cp in page_copies(b, h, slot):
        cp.wait()

    slope = slopes_ref[h]
    qf = q_ref[0]
    qb = qf.astype(_BF16)
    row = lax.broadcasted_iota(jnp.int32, (n_new, n_flat), 0)
    lane = lax.broadcasted_iota(jnp.int32, (n_new, n_flat), 1)

    s_sel = jnp.zeros((n_new, n_flat), _F32)
    base = jnp.zeros((n_new, n_flat), jnp.int32)
    for t in range(n_new):
        k_cat = kbuf[slot, t * per_t:(t + 1) * per_t].reshape(n_flat, HEAD_DIM).astype(_BF16)
        s_t = lax.dot_general(qb, k_cat, _NT, preferred_element_type=_F32)
        s_sel = jnp.where(row == t, s_t, s_sel)
        for c in range(per_t):
            blk_idx = idx_ref[b, (h * n_new + t) * MOBA_TOPK + c // PAGES_PER_BLOCK]
            start = blk_idx * MOBA_BLOCK + (c % PAGES_PER_BLOCK) * PAGE_SIZE - c * PAGE_SIZE
            in_page = (row == t) & (lane >= c * PAGE_SIZE) & (lane < (c + 1) * PAGE_SIZE)
            base = jnp.where(in_page, start, base)
    dist_sel = (past + row - (base + lane)).astype(_F32)
    s_sel = s_sel * SCALE - slope * dist_sel

    lane_o = lax.broadcasted_iota(jnp.int32, (n_new, 128), 1)
    row_o = lax.broadcasted_iota(jnp.int32, (n_new, 128), 0)
    knf = kn_ref[0]
    s_own = jnp.full((n_new, 128), NEG, _F32)
    for t2 in range(n_new):
        dots = jnp.sum(qf * knf[t2:t2 + 1, :], axis=-1, keepdims=True)
        s_own = jnp.where(lane_o == t2, dots, s_own)
    dist_own = (row_o - lane_o).astype(_F32)
    s_own = jnp.where((lane_o < n_new) & (row_o >= lane_o), s_own * SCALE - slope * dist_own, NEG)

    m = jnp.maximum(jnp.max(s_sel, axis=-1, keepdims=True), jnp.max(s_own, axis=-1, keepdims=True))
    p_sel = jnp.exp(s_sel - m)
    p_own = jnp.exp(s_own - m)
    l = jnp.sum(p_sel, axis=-1, keepdims=True) + jnp.sum(p_own, axis=-1, keepdims=True)
    p_sel = (p_sel / l).astype(_BF16)
    p_own = p_own / l

    row_d = lax.broadcasted_iota(jnp.int32, (n_new, HEAD_DIM), 0)
    o = jnp.zeros((n_new, HEAD_DIM), _F32)
    for t in range(n_new):
        v_cat = vbuf[slot, t * per_t:(t + 1) * per_t].reshape(n_flat, HEAD_DIM).astype(_BF16)
        o_t = jnp.dot(p_sel, v_cat, preferred_element_type=_F32)
        o = jnp.where(row_d == t, o_t, o)
    vnf = vn_ref[0]
    for t2 in range(n_new):
        w = jnp.sum(jnp.where(lane_o == t2, p_own, 0.0), axis=-1, keepdims=True)
        o = o + w * vnf[t2:t2 + 1, :]
    o_ref[0] = (o * _silu(zb_ref[0].astype(_F32))).astype(o_ref.dtype)


def _moba_dec(page_table, idx, slopes, q3, k3, v3, zb3, cache_k, cache_v, *, past):
    n_bat, n_new, _ = q3.shape
    n_pg = n_new * MOBA_TOPK * PAGES_PER_BLOCK
    tok = pl.BlockSpec((1, n_new, HEAD_DIM), lambda b, h, pt, ix: (b, 0, h))
    hbm = pl.BlockSpec(memory_space=pl.ANY)
    grid_spec = pltpu.PrefetchScalarGridSpec(
        num_scalar_prefetch=2,
        grid=(n_bat, N_HEADS),
        in_specs=[pl.BlockSpec(memory_space=pltpu.SMEM), tok, tok, tok, tok, hbm, hbm],
        out_specs=tok,
        scratch_shapes=[pltpu.VMEM((2, n_pg, PAGE_SIZE, HEAD_DIM), _F32),
                        pltpu.VMEM((2, n_pg, PAGE_SIZE, HEAD_DIM), _F32),
                        pltpu.SemaphoreType.DMA((2,))],
    )
    return pl.pallas_call(
        functools.partial(_moba_dec_kernel, n_new=n_new, past=past),
        grid_spec=grid_spec,
        out_shape=jax.ShapeDtypeStruct((n_bat, n_new, D_B), _BF16),
        compiler_params=pltpu.CompilerParams(
            dimension_semantics=("arbitrary", "arbitrary"), vmem_limit_bytes=VMEM_LIMIT),
        name="moba_dec",
    )(page_table, idx, slopes, q3, k3, v3, zb3, cache_k, cache_v)


def _merge_kernel(x_ref, ya_ref, yb_ref, ga_ref, gb_ref, woa_ref, wob_ref, wo_ref, y_ref):
    br_a = jnp.dot(ya_ref[...], woa_ref[...], preferred_element_type=_F32)
    br_b = jnp.dot(yb_ref[...], wob_ref[...], preferred_element_type=_F32)
    mix = (jax.nn.sigmoid(ga_ref[...].astype(_F32)) * br_a
           + jax.nn.sigmoid(gb_ref[...].astype(_F32)) * br_b)
    y_ref[...] = x_ref[...] + jnp.dot(mix.astype(_BF16), wo_ref[...], preferred_element_type=_F32)


def _merge(x2d, y_a, y_b, pb, w_oa, w_ob, w_o, *, tm):
    m = x2d.shape[0]

    def rows(width, col=0):
        return pl.BlockSpec((tm, width), lambda i: (i, col))

    def whole(shape):
        return pl.BlockSpec(shape, lambda i: (0, 0), pipeline_mode=pl.Buffered(1))

    return pl.pallas_call(
        _merge_kernel,
        grid=(m // tm,),
        in_specs=[rows(D_MODEL), rows(D_A), rows(D_B),
                  rows(D_MODEL, P_GA * SECTION // D_MODEL), rows(D_MODEL, P_GB * SECTION // D_MODEL),
                  whole(w_oa.shape), whole(w_ob.shape), whole(w_o.shape)],
        out_specs=rows(D_MODEL),
        out_shape=jax.ShapeDtypeStruct((m, D_MODEL), _F32),
        compiler_params=pltpu.CompilerParams(
            dimension_semantics=("arbitrary",), vmem_limit_bytes=VMEM_LIMIT),
        name="merge",
    )(x2d, y_a, y_b, pb, pb, w_oa, w_ob, w_o)


def kernel(x_prompt, x_sample, cache_k, cache_v, page_table, norm_w, w_in, v_norm_w, q_norm_w,
           k_norm_w, w_spatial, b_spatial, w_out_a, w_out_b, w_out):
    depth = norm_w.shape[0]
    assert depth == 1, "single-layer trunk"
    n_bp, seq, _ = x_prompt.shape
    assert n_bp == 1 and seq % MOBA_BLOCK == 0
    n_bat, n_new, _ = x_sample.shape
    n_pages = page_table.shape[1]
    past = n_pages * PAGE_SIZE
    assert past % MOBA_BLOCK == 0 and n_pages // PAGES_PER_BLOCK >= MOBA_TOPK and n_new <= CHUNK
    layer = 0

    slopes = jnp.asarray(np.array([2.0 ** (-8.0 * (h + 1) / N_HEADS) for h in range(N_HEADS)], np.float32))
    ones = jnp.ones((SECTION,), _F32)
    epi_w = jnp.concatenate([
        ones, v_norm_w[layer], ones, jnp.tile(q_norm_w[layer], N_HEADS), jnp.tile(k_norm_w[layer], N_HEADS),
        ones, ones, ones, ones, ones, ones]).reshape(1, D_IN)
    nw = norm_w[layer].reshape(1, D_MODEL)
    w_in_b = w_in[layer].astype(_BF16)
    w_oa_b = w_out_a[layer].astype(_BF16)
    w_ob_b = w_out_b[layer].astype(_BF16)
    w_o_b = w_out[layer].astype(_BF16)

    xp = x_prompt.reshape(seq, D_MODEL)
    pf, pb = _project_both(xp, nw, w_in_b, epi_w, tm=1024, tn=512)
    y_a = _gmlp(pf, pb, w_spatial[layer], b_spatial[layer].T, tm=512)
    y_b = _moba(pf, pb, slopes, sub=4)
    y_prompt = _merge(xp, y_a, y_b, pb, w_oa_b, w_ob_b, w_o_b, tm=256)

    def section(arr, c):
        return arr[:, c * SECTION:(c + 1) * SECTION]

    k, vb = section(pf, F_K), section(pf, F_VB)

    m_s = n_bat * n_new
    xs = x_sample.reshape(m_s, D_MODEL)
    pf_s, pb_s = _project_both(xs, nw, w_in_b, epi_w, tm=m_s, tn=512)
    v_s, q_s, k_s, vb_s = (section(pf_s, c) for c in (F_V, F_Q, F_K, F_VB))
    u_s, za_s, zb_s = (section(pb_s, c) for c in (P_U, P_ZA, P_ZB))
    w4 = w_spatial[layer][:, :n_new, :n_new]
    wrow = jnp.repeat(w4.transpose(1, 2, 0), GROUP_DIM, axis=-1).reshape(n_new * n_new, D_A)
    brow = jnp.repeat(b_spatial[layer][:, :n_new].T, GROUP_DIM, axis=-1)
    three = (n_bat, n_new, D_A)
    ya_s = _gmlp_dec(u_s.reshape(three), v_s.reshape(three), za_s.reshape(three), wrow, brow)
    q3 = q_s.reshape(n_bat, n_new, D_B)
    idx_pad = _kmean_topk(page_table, q3, cache_k[layer], pages_per_step=16)
    idx = idx_pad[:, :, :MOBA_TOPK].reshape(n_bat, N_HEADS * n_new * MOBA_TOPK)
    yb_s = _moba_dec(page_table, idx, slopes, q3, k_s.reshape(n_bat, n_new, D_B),
                     vb_s.reshape(n_bat, n_new, D_B), zb_s.reshape(n_bat, n_new, D_B),
                     cache_k[layer], cache_v[layer], past=past)
    y_sample = _merge(xs, ya_s.reshape(m_s, D_A), yb_s.reshape(m_s, D_B), pb_s,
                      w_oa_b, w_ob_b, w_o_b, tm=m_s)

    return (y_prompt.reshape(x_prompt.shape),
            y_sample.reshape(x_sample.shape),
            k.reshape(depth, n_bp, seq, N_HEADS, HEAD_DIM),
            vb.reshape(depth, n_bp, seq, N_HEADS, HEAD_DIM),
            k_s.reshape(depth, n_bat, n_new, N_HEADS, HEAD_DIM),
            vb_s.reshape(depth, n_bat, n_new, N_HEADS, HEAD_DIM),
            v_s.reshape(depth, n_bat, n_new, D_A))
```

```python
import functools

import jax
import jax.numpy as jnp
import numpy as np
from jax import lax
from jax.experimental import pallas as pl
from jax.experimental.pallas import tpu as pltpu

D_MODEL = 2048
A_GROUPS = 8
GROUP_DIM = 128
D_A = A_GROUPS * GROUP_DIM
CHUNK = 128
N_HEADS = 8
HEAD_DIM = 128
D_B = N_HEADS * HEAD_DIM
MOBA_BLOCK = 256
MOBA_TOPK = 3
PAGE_SIZE = 128
PAGES_PER_BLOCK = MOBA_BLOCK // PAGE_SIZE
DEC_SLOTS = 3
SECTION = 1024
N_SECTIONS = 11
D_IN = N_SECTIONS * SECTION
EPS = 1e-6
NEG = -1e30
REMOVED = -3e38
SCALE = HEAD_DIM ** -0.5
VMEM_LIMIT = 56 * 1024 * 1024

_NT = (((1,), (1,)), ((), ()))
_BF16 = jnp.bfloat16
_F32 = jnp.float32


def _silu(z):
    return z * jax.nn.sigmoid(z)


W_SECTIONS_F32 = (1, 3, 4, 5)
NORMED_F32 = (1, 1, 1, 0)
W_SECTIONS_BF16 = (7, 8, 9, 10, 0, 2, 6)
F_V, F_Q, F_K, F_VB = 0, 1, 2, 3
P_GA, P_GB, P_U, P_ZA, P_ZB = 0, 2, 4, 5, 6


def _proj_kernel(wcol_ref, normed_ref, pages_ref, x_ref, nw_ref, w_ref, ew_ref, *rest,
                 tn, grouped, first_page, pages_per_step):
    if pages_per_step:
        ck_ref, o_ref, km_ref, h_ref, pbuf, sems = rest
    else:
        o_ref, h_ref = rest
    j = pl.program_id(1)

    if pages_per_step:
        n_j = pl.num_programs(1)
        step = pl.program_id(0) * n_j + j
        slot = step % 2

        def page_copies(st, sl):
            base = first_page + st * pages_per_step
            return [pltpu.make_async_copy(ck_ref.at[pages_ref[base + r]], pbuf.at[sl, r], sems.at[sl])
                    for r in range(pages_per_step)]

        @pl.when(step == 0)
        def _():
            for cp in page_copies(step, slot):
                cp.start()

        for cp in page_copies(step, slot):
            cp.wait()

        @pl.when(step + 1 < pl.num_programs(0) * n_j)
        def _():
            for cp in page_copies(step + 1, 1 - slot):
                cp.start()

    @pl.when(j == 0)
    def _():
        xf = x_ref[...]
        ms = jnp.mean(xf * xf, axis=-1, keepdims=True)
        h_ref[...] = (xf * lax.rsqrt(ms + EPS) * nw_ref[...]).astype(_BF16)

    acc = jnp.dot(h_ref[...], w_ref[...], preferred_element_type=_F32)
    if grouped:
        use_norm = normed_ref[j] != 0
        for c in range(tn // GROUP_DIM):
            cs = slice(c * GROUP_DIM, (c + 1) * GROUP_DIM)
            blk = acc[:, cs]
            ms = jnp.mean(blk * blk, axis=-1, keepdims=True)
            scale = jnp.where(use_norm, lax.rsqrt(ms + EPS), 1.0)
            o_ref[:, cs] = (blk * scale * ew_ref[:, cs]).astype(o_ref.dtype)
    else:
        o_ref[...] = acc.astype(o_ref.dtype)

    for blk_i in range(pages_per_step // PAGES_PER_BLOCK):
        tot = jnp.sum(pbuf[slot, PAGES_PER_BLOCK * blk_i], axis=0)
        for r in range(1, PAGES_PER_BLOCK):
            tot = tot + jnp.sum(pbuf[slot, PAGES_PER_BLOCK * blk_i + r], axis=0)
        km_ref[blk_i] = tot * (1.0 / MOBA_BLOCK)


def _project(x2d, norm_w, w_bf16, epi_w, *, sections, normed, out_dtype, tm, tn, stream=None):
    m = x2d.shape[0]
    tiles = SECTION // tn
    n_tiles = len(sections) * tiles
    n_steps = (m // tm) * n_tiles
    wcol = np.array([s * tiles + t for s in sections for t in range(tiles)], np.int32)
    flags = np.array([f for f in (normed or (0,) * len(sections)) for _ in range(tiles)], np.int32)
    in_specs = [
        pl.BlockSpec((tm, D_MODEL), lambda i, j, *_: (i, 0)),
        pl.BlockSpec((1, D_MODEL), lambda i, j, *_: (0, 0)),
        pl.BlockSpec((D_MODEL, tn), lambda i, j, wc, *_: (0, wc[j])),
        pl.BlockSpec((1, tn), lambda i, j, wc, *_: (0, wc[j])),
    ]
    out_specs = [pl.BlockSpec((tm, tn), lambda i, j, *_: (i, j))]
    out_shape = [jax.ShapeDtypeStruct((m, len(sections) * SECTION), out_dtype)]
    scratch = [pltpu.VMEM((tm, D_MODEL), _BF16)]
    operands = [x2d, norm_w, w_bf16, epi_w]
    if stream is None:
        pages, first_page, pages_per_step = jnp.zeros((1,), jnp.int32), 0, 0
    else:
        pages, cache_k, first_page, pages_per_step = stream
        assert pages_per_step % PAGES_PER_BLOCK == 0 and first_page % PAGES_PER_BLOCK == 0
        blocks_per_step = pages_per_step // PAGES_PER_BLOCK
        in_specs.append(pl.BlockSpec(memory_space=pl.ANY))
        operands.append(cache_k)
        out_specs.append(pl.BlockSpec((blocks_per_step, N_HEADS, HEAD_DIM), lambda i, j, *_: (i * n_tiles + j, 0, 0)))
        out_shape.append(jax.ShapeDtypeStruct((n_steps * blocks_per_step, N_HEADS, HEAD_DIM), _F32))
        scratch += [pltpu.VMEM((2, pages_per_step, PAGE_SIZE, N_HEADS, HEAD_DIM), _F32),
                    pltpu.SemaphoreType.DMA((2,))]
    grid_spec = pltpu.PrefetchScalarGridSpec(
        num_scalar_prefetch=3, grid=(m // tm, n_tiles),
        in_specs=in_specs, out_specs=out_specs, scratch_shapes=scratch)
    outs = pl.pallas_call(
        functools.partial(_proj_kernel, tn=tn, grouped=normed is not None,
                          first_page=first_page, pages_per_step=pages_per_step),
        grid_spec=grid_spec,
        out_shape=out_shape,
        compiler_params=pltpu.CompilerParams(
            dimension_semantics=("arbitrary", "arbitrary"), vmem_limit_bytes=VMEM_LIMIT),
        name="proj_f32" if normed is not None else "proj_bf16",
    )(jnp.asarray(wcol), jnp.asarray(flags), pages, *operands)
    return outs if stream is not None else outs[0]


def _project_both(x2d, norm_w, w_bf16, epi_w, *, tm, tn, stream=None):
    common = dict(tm=tm, tn=tn)
    f32_args = dict(sections=W_SECTIONS_F32, normed=NORMED_F32, out_dtype=_F32, **common)
    bf16_args = dict(sections=W_SECTIONS_BF16, normed=None, out_dtype=_BF16, **common)
    if stream is None:
        return _project(x2d, norm_w, w_bf16, epi_w, **f32_args), _project(x2d, norm_w, w_bf16, epi_w, **bf16_args)
    pages, cache_k = stream
    row_blocks, tiles = x2d.shape[0] // tm, SECTION // tn
    steps_f = row_blocks * len(W_SECTIONS_F32) * tiles
    steps_b = row_blocks * len(W_SECTIONS_BF16) * tiles
    n_pages = pages.shape[0]
    per_f = (n_pages // (steps_f + steps_b)) // PAGES_PER_BLOCK * PAGES_PER_BLOCK
    per_b, rem = divmod(n_pages - per_f * steps_f, steps_b)
    assert rem == 0 and per_b % PAGES_PER_BLOCK == 0, "page count must split evenly over the projection steps"
    pf, km_f = _project(x2d, norm_w, w_bf16, epi_w, stream=(pages, cache_k, 0, per_f), **f32_args)
    pb, km_b = _project(x2d, norm_w, w_bf16, epi_w, stream=(pages, cache_k, per_f * steps_f, per_b), **bf16_args)
    return pf, pb, jnp.concatenate([km_f, km_b], axis=0)


def _gmlp_kernel(u_ref, v_ref, za_ref, ws_ref, bs_ref, ya_ref, *, n_chunks):
    row = lax.broadcasted_iota(jnp.int32, (CHUNK, CHUNK), 0)
    col = lax.broadcasted_iota(jnp.int32, (CHUNK, CHUNK), 1)
    tril = row >= col
    for g in range(A_GROUPS):
        w_g = jnp.where(tril, ws_ref[g], 0.0).astype(_BF16)
        b_g = bs_ref[:, g:g + 1]
        cs = slice(g * GROUP_DIM, (g + 1) * GROUP_DIM)
        for c in range(n_chunks):
            rs = slice(c * CHUNK, (c + 1) * CHUNK)
            mixed = jnp.dot(w_g, v_ref[rs, cs].astype(_BF16), preferred_element_type=_F32) + b_g
            y = u_ref[rs, cs].astype(_F32) * mixed * _silu(za_ref[rs, cs].astype(_F32))
            ya_ref[rs, cs] = y.astype(ya_ref.dtype)


def _gmlp(pf, pb, w_spatial, b_spatial_t, *, tm):
    m = pf.shape[0]

    def section(c):
        return pl.BlockSpec((tm, D_A), lambda i: (i, c))

    row_spec = section(0)
    return pl.pallas_call(
        functools.partial(_gmlp_kernel, n_chunks=tm // CHUNK),
        grid=(m // tm,),
        in_specs=[section(P_U), section(F_V), section(P_ZA),
                  pl.BlockSpec((A_GROUPS, CHUNK, CHUNK), lambda i: (0, 0, 0)),
                  pl.BlockSpec((CHUNK, A_GROUPS), lambda i: (0, 0))],
        out_specs=row_spec,
        out_shape=jax.ShapeDtypeStruct((m, D_A), _BF16),
        compiler_params=pltpu.CompilerParams(
            dimension_semantics=("arbitrary",), vmem_limit_bytes=VMEM_LIMIT),
        name="gmlp",
    )(pb, pf, pb, w_spatial, b_spatial_t)


def _gmlp_dec_kernel(u_ref, v_ref, za_ref, wrow_ref, brow_ref, ya_ref, *, n_new):
    for t in range(n_new):
        mixed = brow_ref[t:t + 1, :]
        for s in range(t + 1):
            mixed = mixed + wrow_ref[t * n_new + s:t * n_new + s + 1, :] * v_ref[:, s, :]
        y = u_ref[:, t, :].astype(_F32) * mixed * _silu(za_ref[:, t, :].astype(_F32))
        ya_ref[:, t, :] = y.astype(ya_ref.dtype)


def _gmlp_dec(u, v, z_a, wrow, brow):
    n_bat, n_new, _ = v.shape
    full3 = pl.BlockSpec((n_bat, n_new, D_A), lambda i: (0, 0, 0))
    return pl.pallas_call(
        functools.partial(_gmlp_dec_kernel, n_new=n_new),
        grid=(1,),
        in_specs=[full3, full3, full3,
                  pl.BlockSpec(wrow.shape, lambda i: (0, 0)),
                  pl.BlockSpec(brow.shape, lambda i: (0, 0))],
        out_specs=full3,
        out_shape=jax.ShapeDtypeStruct((n_bat, n_new, D_A), _BF16),
        name="gmlp_dec",
    )(u, v, z_a, wrow, brow)


def _topk_mask(gate, col, n_valid, n_cols):
    g = jnp.where(col < n_valid, gate, NEG)
    sel = jnp.zeros(gate.shape, jnp.bool_)
    for j in range(MOBA_TOPK):
        m = jnp.max(g, axis=-1, keepdims=True)
        idx = jnp.min(jnp.where(g == m, col, n_cols), axis=-1, keepdims=True)
        pick = col == idx
        sel = sel | (pick & (j < n_valid))
        g = jnp.where(pick, REMOVED, g)
    return sel


def _bf16_pieces(x, n):
    out = []
    for _ in range(n):
        piece = float(np.asarray(x, np.float32).astype(_BF16).astype(np.float32))
        out.append(piece)
        x = x - piece
    return tuple(out)


_LOG2E = 1.4426950408889634
_LOG2E_PIECES = _bf16_pieces(_LOG2E, 3)
_N_PIECES = len(_LOG2E_PIECES)


def _lane_values(lane, first, values):
    out = jnp.zeros(lane.shape, _F32)
    for p, val in enumerate(values):
        out = jnp.where(lane == first + p, val, out)
    return out


def _moba_kernel(slopes_ref, q_ref, k_ref, v_ref, zb_ref, o_ref,
                 kaug_ref, vb_ref, km_ref, qa_ref, m_ref, l_ref, acc_ref, sa_ref, sb_ref, *, n_blocks, sub):
    h = pl.program_id(0)
    qi = pl.program_id(1)
    blk = MOBA_BLOCK
    lane_hi = n_blocks
    lane_lo = lane_hi + _N_PIECES
    lane_t0 = lane_lo + _N_PIECES
    slope = slopes_ref[h]
    coef = [slope * c for c in _LOG2E_PIECES]
    lane = lax.broadcasted_iota(jnp.int32, (blk, HEAD_DIM), 1)
    hi_lanes = (lane >= lane_hi) & (lane < lane_lo)

    @pl.when(qi == 0)
    def _():
        km_ref[...] = jnp.zeros(km_ref.shape, _F32)
        row_f = lax.broadcasted_iota(jnp.int32, (blk, HEAD_DIM), 0).astype(_F32)
        fixed = (_lane_values(lane, lane_t0, [-c * blk for c in coef])
                 + jnp.where((lane >= lane_lo) & (lane < lane_t0), row_f, 0.0))

        def body(n, carry):
            rows = pl.ds(pl.multiple_of(n * blk, blk), blk)
            kf = k_ref[rows, :]
            aug = fixed + jnp.where(lane == n, NEG, 0.0) + jnp.where(hi_lanes, jnp.asarray(n * blk, _F32), 0.0)
            kaug_ref[rows, :HEAD_DIM] = kf.astype(_BF16)
            kaug_ref[rows, HEAD_DIM:] = aug.astype(_BF16)
            vb_ref[rows, :] = v_ref[rows, :].astype(_BF16)
            km_ref[pl.ds(n, 1), :] = jnp.mean(kf, axis=0, keepdims=True)
            return carry
        lax.fori_loop(0, n_blocks, body, 0)

    q_consts = (_lane_values(lane, lane_hi, coef) + _lane_values(lane, lane_lo, coef)
                + jnp.where((lane >= lane_t0) & (lane < lane_t0 + _N_PIECES), jnp.asarray(qi * sub, _F32), 0.0))
    for r in range(sub):
        rs = slice(r * blk, (r + 1) * blk)
        qf = q_ref[rs, :]
        gate = lax.dot_general(qf, km_ref[...], _NT, precision=lax.Precision.HIGHEST,
                               preferred_element_type=_F32)
        own = qi * sub + r
        open_blocks = _topk_mask(gate, lane, own, HEAD_DIM) | (lane == own)
        qa_ref[rs, :HEAD_DIM] = (qf * (SCALE * _LOG2E)).astype(_BF16)
        qa_ref[rs, HEAD_DIM:] = jnp.where(lane < n_blocks, jnp.where(open_blocks, 0.0, 1.0), q_consts).astype(_BF16)

    def scores(r, k_tile):
        rs = slice(r * blk, (r + 1) * blk)
        return lax.dot_general(qa_ref[rs, :], k_tile, _NT, preferred_element_type=_F32)

    def attend(r, s, v_tile, first):
        rs = slice(r * blk, (r + 1) * blk)
        s0, s1 = s[:, :HEAD_DIM], s[:, HEAD_DIM:]
        mx = jnp.max(jnp.maximum(s0, s1), axis=-1, keepdims=True)
        if first:
            m_new = jnp.broadcast_to(mx, (blk, HEAD_DIM))
        else:
            m_old = m_ref[rs, :]
            m_new = jnp.maximum(m_old, mx)
        p0 = jnp.exp2(s0 - m_new)
        p1 = jnp.exp2(s1 - m_new)
        row_sum = jnp.sum(p0 + p1, axis=-1, keepdims=True)
        pv = jnp.dot(jnp.concatenate([p0, p1], axis=-1).astype(_BF16), v_tile, preferred_element_type=_F32)
        if first:
            l_ref[rs, :] = jnp.broadcast_to(row_sum, (blk, HEAD_DIM))
            acc_ref[rs, :] = pv
        else:
            alpha = jnp.exp2(m_old - m_new)
            l_ref[rs, :] = alpha * l_ref[rs, :] + row_sum
            acc_ref[rs, :] = alpha * acc_ref[rs, :] + pv
        m_ref[rs, :] = m_new

    ri = lax.broadcasted_iota(jnp.int32, (blk, blk), 0)
    ci = lax.broadcasted_iota(jnp.int32, (blk, blk), 1)
    causal = ri >= ci
    for r in range(sub):
        for r2 in [r] + list(range(r)):
            rows = pl.ds(pl.multiple_of((qi * sub + r2) * blk, blk), blk)
            s = scores(r, kaug_ref[rows, :])
            attend(r, jnp.where(causal, s, NEG) if r2 == r else s, vb_ref[rows, :], first=(r2 == r))

    n_past = qi * sub

    def block_rows(n):
        return pl.ds(pl.multiple_of(n * blk, blk), blk)

    def produce(n, dst_ref):
        k_tile = kaug_ref[block_rows(n), :]
        for r in range(sub):
            dst_ref[r * blk:(r + 1) * blk, :] = scores(r, k_tile)

    def consume(n, src_ref):
        v_tile = vb_ref[block_rows(n), :]
        for r in range(sub):
            attend(r, src_ref[r * blk:(r + 1) * blk, :], v_tile, first=False)

    @pl.when(n_past > 0)
    def _():
        produce(0, sa_ref)

    def body(j, carry):
        n0 = 2 * j
        produce(n0 + 1, sb_ref)
        consume(n0, sa_ref)
        produce(jnp.minimum(n0 + 2, n_past - 1), sa_ref)
        consume(n0 + 1, sb_ref)
        return carry
    lax.fori_loop(0, n_past // 2, body, 0)

    o = acc_ref[...] / l_ref[...]
    o_ref[...] = (o * _silu(zb_ref[...].astype(_F32))).astype(o_ref.dtype)


def _moba(pf, pb, slopes, *, sub):
    n_pos = pf.shape[0]
    n_blocks = n_pos // MOBA_BLOCK
    tq = sub * MOBA_BLOCK
    assert n_pos % tq == 0 and sub % 2 == 0 and n_blocks + 3 * _N_PIECES <= HEAD_DIM

    def tile_of(section):
        return pl.BlockSpec((tq, HEAD_DIM), lambda h, i: (i, section * N_HEADS + h))

    def head_all(section):
        return pl.BlockSpec((n_pos, HEAD_DIM), lambda h, i: (0, section * N_HEADS + h))

    return pl.pallas_call(
        functools.partial(_moba_kernel, n_blocks=n_blocks, sub=sub),
        grid=(N_HEADS, n_pos // tq),
        in_specs=[pl.BlockSpec(memory_space=pltpu.SMEM), tile_of(F_Q), head_all(F_K), head_all(F_VB),
                  tile_of(P_ZB)],
        out_specs=tile_of(0),
        out_shape=jax.ShapeDtypeStruct((n_pos, D_B), _BF16),
        scratch_shapes=[pltpu.VMEM((n_pos, 2 * HEAD_DIM), _BF16),
                        pltpu.VMEM((n_pos, HEAD_DIM), _BF16),
                        pltpu.VMEM((HEAD_DIM, HEAD_DIM), _F32),
                        pltpu.VMEM((tq, 2 * HEAD_DIM), _BF16),
                        pltpu.VMEM((tq, HEAD_DIM), _F32),
                        pltpu.VMEM((tq, HEAD_DIM), _F32),
                        pltpu.VMEM((tq, HEAD_DIM), _F32),
                        pltpu.VMEM((tq, MOBA_BLOCK), _F32),
                        pltpu.VMEM((tq, MOBA_BLOCK), _F32)],
        compiler_params=pltpu.CompilerParams(
            dimension_semantics=("arbitrary", "arbitrary"), vmem_limit_bytes=VMEM_LIMIT),
        name="moba",
    )(slopes, pf, pf, pf, pb)


def _gate_kernel(q_ref, km_ref, idx_ref, *, n_full, n_new):
    col = lax.broadcasted_iota(jnp.int32, (n_new, n_full), 1)
    lane = lax.broadcasted_iota(jnp.int32, (n_new, 128), 1)
    for h in range(N_HEADS):
        qh = q_ref[0, :, h * HEAD_DIM:(h + 1) * HEAD_DIM]
        g = lax.dot_general(qh, km_ref[0, :, h, :], _NT, precision=lax.Precision.HIGHEST,
                            preferred_element_type=_F32)
        tile = jnp.zeros((n_new, 128), jnp.int32)
        for j in range(MOBA_TOPK):
            m = jnp.max(g, axis=-1, keepdims=True)
            idx = jnp.min(jnp.where(g == m, col, n_full), axis=-1, keepdims=True)
            tile = jnp.where(lane == j, idx, tile)
            g = jnp.where(col == idx, REMOVED, g)
        idx_ref[0, h * n_new:(h + 1) * n_new, :] = tile


def _gate_topk(q3, k_mean):
    n_bat, n_new, _ = q3.shape
    n_full = k_mean.shape[1]
    return pl.pallas_call(
        functools.partial(_gate_kernel, n_full=n_full, n_new=n_new),
        grid=(n_bat,),
        in_specs=[pl.BlockSpec((1, n_new, D_B), lambda b: (b, 0, 0)),
                  pl.BlockSpec((1, n_full, N_HEADS, HEAD_DIM), lambda b: (b, 0, 0, 0))],
        out_specs=pl.BlockSpec((1, N_HEADS * n_new, 128), lambda b: (b, 0, 0)),
        out_shape=jax.ShapeDtypeStruct((n_bat, N_HEADS * n_new, 128), jnp.int32),
        name="gate",
    )(q3, k_mean)


def _moba_dec_kernel(pt_ref, idx_ref, slopes_ref, q_ref, kn_ref, vn_ref, zb_ref, ck_ref, cv_ref,
                     o_ref, kbuf, vbuf, sems, *, n_new, past):
    per_t = MOBA_TOPK * PAGES_PER_BLOCK
    n_flat = MOBA_TOPK * MOBA_BLOCK
    b = pl.program_id(0)
    h = pl.program_id(1)
    n_heads = pl.num_programs(1)
    step = b * n_heads + h
    n_steps = pl.num_programs(0) * n_heads
    n_slots = kbuf.shape[0]
    ahead = n_slots - 1
    slot = step % n_slots

    def page_copies(st, sl):
        bb, hh = st // n_heads, st % n_heads
        copies = []
        for t in range(n_new):
            for c in range(per_t):
                blk_idx = idx_ref[bb, (hh * n_new + t) * MOBA_TOPK + c // PAGES_PER_BLOCK]
                page = pt_ref[bb, blk_idx * PAGES_PER_BLOCK + c % PAGES_PER_BLOCK]
                copies.append(pltpu.make_async_copy(
                    ck_ref.at[page, :, hh, :], kbuf.at[sl, t * per_t + c], sems.at[sl]))
                copies.append(pltpu.make_async_copy(
                    cv_ref.at[page, :, hh, :], vbuf.at[sl, t * per_t + c], sems.at[sl]))
        return copies

    @pl.when(step == 0)
    def _():
        for st in range(ahead):
            @pl.when(st < n_steps)
            def _(st=st):
                for cp in page_copies(st, st % n_slots):
                    cp.start()

    @pl.when(step + ahead < n_steps)
    def _():
        for cp in page_copies(step + ahead, (step + ahead) % n_slots):
            cp.start()

    for cp in page_copies(step, slot):
        cp.wait()

    slope = slopes_ref[h]
    qf = q_ref[0]
    qb = qf.astype(_BF16)
    row = lax.broadcasted_iota(jnp.int32, (n_new, n_flat), 0)
    lane = lax.broadcasted_iota(jnp.int32, (n_new, n_flat), 1)

    s_sel = jnp.zeros((n_new, n_flat), _F32)
    base = jnp.zeros((n_new, n_flat), jnp.int32)
    for t in range(n_new):
        k_cat = kbuf[slot, t * per_t:(t + 1) * per_t].reshape(n_flat, HEAD_DIM).astype(_BF16)
        s_t = lax.dot_general(qb, k_cat, _NT, preferred_element_type=_F32)
        s_sel = jnp.where(row == t, s_t, s_sel)
        for c in range(per_t):
            blk_idx = idx_ref[b, (h * n_new + t) * MOBA_TOPK + c // PAGES_PER_BLOCK]
            start = blk_idx * MOBA_BLOCK + (c % PAGES_PER_BLOCK) * PAGE_SIZE - c * PAGE_SIZE
            in_page = (row == t) & (lane >= c * PAGE_SIZE) & (lane < (c + 1) * PAGE_SIZE)
            base = jnp.where(in_page, start, base)
    dist_sel = (past + row - (base + lane)).astype(_F32)
    s_sel = s_sel * SCALE - slope * dist_sel

    lane_o = lax.broadcasted_iota(jnp.int32, (n_new, 128), 1)
    row_o = lax.broadcasted_iota(jnp.int32, (n_new, 128), 0)
    knf = kn_ref[0]
    s_own = jnp.full((n_new, 128), NEG, _F32)
    for t2 in range(n_new):
        dots = jnp.sum(qf * knf[t2:t2 + 1, :], axis=-1, keepdims=True)
        s_own = jnp.where(lane_o == t2, dots, s_own)
    dist_own = (row_o - lane_o).astype(_F32)
    s_own = jnp.where((lane_o < n_new) & (row_o >= lane_o), s_own * SCALE - slope * dist_own, NEG)

    m = jnp.maximum(jnp.max(s_sel, axis=-1, keepdims=True), jnp.max(s_own, axis=-1, keepdims=True))
    p_sel = jnp.exp(s_sel - m)
    p_own = jnp.exp(s_own - m)
    l = jnp.sum(p_sel, axis=-1, keepdims=True) + jnp.sum(p_own, axis=-1, keepdims=True)
    p_sel = (p_sel / l).astype(_BF16)
    p_own = p_own / l

    row_d = lax.broadcasted_iota(jnp.int32, (n_new, HEAD_DIM), 0)
    o = jnp.zeros((n_new, HEAD_DIM), _F32)
    for t in range(n_new):
        v_cat = vbuf[slot, t * per_t:(t + 1) * per_t].reshape(n_flat, HEAD_DIM).astype(_BF16)
        o_t = jnp.dot(p_sel, v_cat, preferred_element_type=_F32)
        o = jnp.where(row_d == t, o_t, o)
    vnf = vn_ref[0]
    for t2 in range(n_new):
        w = jnp.sum(jnp.where(lane_o == t2, p_own, 0.0), axis=-1, keepdims=True)
        o = o + w * vnf[t2:t2 + 1, :]
    o_ref[0] = (o * _silu(zb_ref[0].astype(_F32))).astype(o_ref.dtype)


def _moba_dec(page_table, idx, slopes, q3, k3, v3, zb3, cache_k, cache_v, *, past):
    n_bat, n_new, _ = q3.shape
    n_pg = n_new * MOBA_TOPK * PAGES_PER_BLOCK
    tok = pl.BlockSpec((1, n_new, HEAD_DIM), lambda b, h, pt, ix: (b, 0, h))
    hbm = pl.BlockSpec(memory_space=pl.ANY)
    grid_spec = pltpu.PrefetchScalarGridSpec(
        num_scalar_prefetch=2,
        grid=(n_bat, N_HEADS),
        in_specs=[pl.BlockSpec(memory_space=pltpu.SMEM), tok, tok, tok, tok, hbm, hbm],
        out_specs=tok,
        scratch_shapes=[pltpu.VMEM((DEC_SLOTS, n_pg, PAGE_SIZE, HEAD_DIM), _F32),
                        pltpu.VMEM((DEC_SLOTS, n_pg, PAGE_SIZE, HEAD_DIM), _F32),
                        pltpu.SemaphoreType.DMA((DEC_SLOTS,))],
    )
    return pl.pallas_call(
        functools.partial(_moba_dec_kernel, n_new=n_new, past=past),
        grid_spec=grid_spec,
        out_shape=jax.ShapeDtypeStruct((n_bat, n_new, D_B), _BF16),
        compiler_params=pltpu.CompilerParams(
            dimension_semantics=("arbitrary", "arbitrary"), vmem_limit_bytes=VMEM_LIMIT),
        name="moba_dec",
    )(page_table, idx, slopes, q3, k3, v3, zb3, cache_k, cache_v)


def _merge_kernel(x_ref, ya_ref, yb_ref, ga_ref, gb_ref, woa_ref, wob_ref, wo_ref, y_ref):
    br_a = jnp.dot(ya_ref[...], woa_ref[...], preferred_element_type=_F32)
    br_b = jnp.dot(yb_ref[...], wob_ref[...], preferred_element_type=_F32)
    mix = (jax.nn.sigmoid(ga_ref[...].astype(_F32)) * br_a
           + jax.nn.sigmoid(gb_ref[...].astype(_F32)) * br_b)
    y_ref[...] = x_ref[...] + jnp.dot(mix.astype(_BF16), wo_ref[...], preferred_element_type=_F32)


def _merge(x2d, y_a, y_b, pb, w_oa, w_ob, w_o, *, tm):
    m = x2d.shape[0]

    def rows(width, col=0):
        return pl.BlockSpec((tm, width), lambda i: (i, col))

    def whole(shape):
        return pl.BlockSpec(shape, lambda i: (0, 0), pipeline_mode=pl.Buffered(1))

    return pl.pallas_call(
        _merge_kernel,
        grid=(m // tm,),
        in_specs=[rows(D_MODEL), rows(D_A), rows(D_B),
                  rows(D_MODEL, P_GA * SECTION // D_MODEL), rows(D_MODEL, P_GB * SECTION // D_MODEL),
                  whole(w_oa.shape), whole(w_ob.shape), whole(w_o.shape)],
        out_specs=rows(D_MODEL),
        out_shape=jax.ShapeDtypeStruct((m, D_MODEL), _F32),
        compiler_params=pltpu.CompilerParams(
            dimension_semantics=("arbitrary",), vmem_limit_bytes=VMEM_LIMIT),
        name="merge",
    )(x2d, y_a, y_b, pb, pb, w_oa, w_ob, w_o)


def kernel(x_prompt, x_sample, cache_k, cache_v, page_table, norm_w, w_in, v_norm_w, q_norm_w,
           k_norm_w, w_spatial, b_spatial, w_out_a, w_out_b, w_out):
    depth = norm_w.shape[0]
    assert depth == 1, "single-layer trunk"
    n_bp, seq, _ = x_prompt.shape
    assert n_bp == 1 and seq % MOBA_BLOCK == 0
    n_bat, n_new, _ = x_sample.shape
    n_pages = page_table.shape[1]
    past = n_pages * PAGE_SIZE
    assert past % MOBA_BLOCK == 0 and n_pages // PAGES_PER_BLOCK >= MOBA_TOPK and n_new <= CHUNK
    layer = 0

    slopes = jnp.asarray(np.array([2.0 ** (-8.0 * (h + 1) / N_HEADS) for h in range(N_HEADS)], np.float32))
    ones = jnp.ones((SECTION,), _F32)
    epi_w = jnp.concatenate([
        ones, v_norm_w[layer], ones, jnp.tile(q_norm_w[layer], N_HEADS), jnp.tile(k_norm_w[layer], N_HEADS),
        ones, ones, ones, ones, ones, ones]).reshape(1, D_IN)
    nw = norm_w[layer].reshape(1, D_MODEL)
    w_in_b = w_in[layer].astype(_BF16)
    w_oa_b = w_out_a[layer].astype(_BF16)
    w_ob_b = w_out_b[layer].astype(_BF16)
    w_o_b = w_out[layer].astype(_BF16)

    xp = x_prompt.reshape(seq, D_MODEL)
    pf, pb, k_mean = _project_both(xp, nw, w_in_b, epi_w, tm=1024, tn=512,
                                   stream=(page_table.reshape(-1), cache_k[layer]))
    y_a = _gmlp(pf, pb, w_spatial[layer], b_spatial[layer].T, tm=512)
    y_b = _moba(pf, pb, slopes, sub=4)
    y_prompt = _merge(xp, y_a, y_b, pb, w_oa_b, w_ob_b, w_o_b, tm=256)

    def section(arr, c):
        return arr[:, c * SECTION:(c + 1) * SECTION]

    k, vb = section(pf, F_K), section(pf, F_VB)

    m_s = n_bat * n_new
    xs = x_sample.reshape(m_s, D_MODEL)
    pf_s, pb_s = _project_both(xs, nw, w_in_b, epi_w, tm=m_s, tn=512)
    v_s, q_s, k_s, vb_s = (section(pf_s, c) for c in (F_V, F_Q, F_K, F_VB))
    u_s, za_s, zb_s = (section(pb_s, c) for c in (P_U, P_ZA, P_ZB))
    w4 = w_spatial[layer][:, :n_new, :n_new]
    wrow = jnp.repeat(w4.transpose(1, 2, 0), GROUP_DIM, axis=-1).reshape(n_new * n_new, D_A)
    brow = jnp.repeat(b_spatial[layer][:, :n_new].T, GROUP_DIM, axis=-1)
    three = (n_bat, n_new, D_A)
    ya_s = _gmlp_dec(u_s.reshape(three), v_s.reshape(three), za_s.reshape(three), wrow, brow)
    q3 = q_s.reshape(n_bat, n_new, D_B)
    n_full = n_pages // PAGES_PER_BLOCK
    idx_pad = _gate_topk(q3, k_mean.reshape(n_bat, n_full, N_HEADS, HEAD_DIM))
    idx = idx_pad[:, :, :MOBA_TOPK].reshape(n_bat, N_HEADS * n_new * MOBA_TOPK)
    yb_s = _moba_dec(page_table, idx, slopes, q3, k_s.reshape(n_bat, n_new, D_B),
                     vb_s.reshape(n_bat, n_new, D_B), zb_s.reshape(n_bat, n_new, D_B),
                     cache_k[layer], cache_v[layer], past=past)
    y_sample = _merge(xs, ya_s.reshape(m_s, D_A), yb_s.reshape(m_s, D_B), pb_s,
                      w_oa_b, w_ob_b, w_o_b, tm=m_s)

    return (y_prompt.reshape(x_prompt.shape),
            y_sample.reshape(x_sample.shape),
            k.reshape(depth, n_bp, seq, N_HEADS, HEAD_DIM),
            vb.reshape(depth, n_bp, seq, N_HEADS, HEAD_DIM),
            k_s.reshape(depth, n_bat, n_new, N_HEADS, HEAD_DIM),
            vb_s.reshape(depth, n_bat, n_new, N_HEADS, HEAD_DIM),
            v_s.reshape(depth, n_bat, n_new, D_A))
```

```python
import functools

import jax
import jax.numpy as jnp
import numpy as np
from jax import lax
from jax.experimental import pallas as pl
from jax.experimental.pallas import tpu as pltpu

D_MODEL = 2048
A_GROUPS = 8
GROUP_DIM = 128
D_A = A_GROUPS * GROUP_DIM
CHUNK = 128
N_HEADS = 8
HEAD_DIM = 128
D_B = N_HEADS * HEAD_DIM
MOBA_BLOCK = 256
MOBA_TOPK = 3
PAGE_SIZE = 128
PAGES_PER_BLOCK = MOBA_BLOCK // PAGE_SIZE
DEC_SLOTS = 3
MOBA_PAGES_PER_ITER = 4
SECTION = 1024
N_SECTIONS = 11
D_IN = N_SECTIONS * SECTION
EPS = 1e-6
NEG = -1e30
REMOVED = -3e38
SCALE = HEAD_DIM ** -0.5
VMEM_LIMIT = 56 * 1024 * 1024

_NT = (((1,), (1,)), ((), ()))
_BF16 = jnp.bfloat16
_F32 = jnp.float32


def _silu(z):
    return z * jax.nn.sigmoid(z)


W_SECTIONS_F32 = (1, 3, 4, 5)
NORMED_F32 = (1, 1, 1, 0)
W_SECTIONS_BF16 = (7, 8, 9, 10, 0, 2, 6)
F_V, F_Q, F_K, F_VB = 0, 1, 2, 3
P_GA, P_GB, P_U, P_ZA, P_ZB = 0, 2, 4, 5, 6


def _proj_kernel(wcol_ref, normed_ref, pages_ref, x_ref, nw_ref, w_ref, ew_ref, *rest,
                 tn, grouped, first_page, pages_per_step):
    if pages_per_step:
        ck_ref, o_ref, km_ref, h_ref, pbuf, sems = rest
    else:
        o_ref, h_ref = rest
    j = pl.program_id(1)

    if pages_per_step:
        n_j = pl.num_programs(1)
        step = pl.program_id(0) * n_j + j
        slot = step % 2

        def page_copies(st, sl):
            base = first_page + st * pages_per_step
            return [pltpu.make_async_copy(ck_ref.at[pages_ref[base + r]], pbuf.at[sl, r], sems.at[sl])
                    for r in range(pages_per_step)]

        @pl.when(step == 0)
        def _():
            for cp in page_copies(step, slot):
                cp.start()

        for cp in page_copies(step, slot):
            cp.wait()

        @pl.when(step + 1 < pl.num_programs(0) * n_j)
        def _():
            for cp in page_copies(step + 1, 1 - slot):
                cp.start()

    @pl.when(j == 0)
    def _():
        xf = x_ref[...]
        ms = jnp.mean(xf * xf, axis=-1, keepdims=True)
        h_ref[...] = (xf * lax.rsqrt(ms + EPS) * nw_ref[...]).astype(_BF16)

    acc = jnp.dot(h_ref[...], w_ref[...], preferred_element_type=_F32)
    if grouped:
        use_norm = normed_ref[j] != 0
        for c in range(tn // GROUP_DIM):
            cs = slice(c * GROUP_DIM, (c + 1) * GROUP_DIM)
            blk = acc[:, cs]
            ms = jnp.mean(blk * blk, axis=-1, keepdims=True)
            scale = jnp.where(use_norm, lax.rsqrt(ms + EPS), 1.0)
            o_ref[:, cs] = (blk * scale * ew_ref[:, cs]).astype(o_ref.dtype)
    else:
        o_ref[...] = acc.astype(o_ref.dtype)

    for blk_i in range(pages_per_step // PAGES_PER_BLOCK):
        tot = jnp.sum(pbuf[slot, PAGES_PER_BLOCK * blk_i], axis=0)
        for r in range(1, PAGES_PER_BLOCK):
            tot = tot + jnp.sum(pbuf[slot, PAGES_PER_BLOCK * blk_i + r], axis=0)
        km_ref[blk_i] = tot * (1.0 / MOBA_BLOCK)


def _project(x2d, norm_w, w_bf16, epi_w, *, sections, normed, out_dtype, tm, tn, stream=None):
    m = x2d.shape[0]
    tiles = SECTION // tn
    n_tiles = len(sections) * tiles
    n_steps = (m // tm) * n_tiles
    wcol = np.array([s * tiles + t for s in sections for t in range(tiles)], np.int32)
    flags = np.array([f for f in (normed or (0,) * len(sections)) for _ in range(tiles)], np.int32)
    in_specs = [
        pl.BlockSpec((tm, D_MODEL), lambda i, j, *_: (i, 0)),
        pl.BlockSpec((1, D_MODEL), lambda i, j, *_: (0, 0)),
        pl.BlockSpec((D_MODEL, tn), lambda i, j, wc, *_: (0, wc[j])),
        pl.BlockSpec((1, tn), lambda i, j, wc, *_: (0, wc[j])),
    ]
    out_specs = [pl.BlockSpec((tm, tn), lambda i, j, *_: (i, j))]
    out_shape = [jax.ShapeDtypeStruct((m, len(sections) * SECTION), out_dtype)]
    scratch = [pltpu.VMEM((tm, D_MODEL), _BF16)]
    operands = [x2d, norm_w, w_bf16, epi_w]
    if stream is None:
        pages, first_page, pages_per_step = jnp.zeros((1,), jnp.int32), 0, 0
    else:
        pages, cache_k, first_page, pages_per_step = stream
        assert pages_per_step % PAGES_PER_BLOCK == 0 and first_page % PAGES_PER_BLOCK == 0
        blocks_per_step = pages_per_step // PAGES_PER_BLOCK
        in_specs.append(pl.BlockSpec(memory_space=pl.ANY))
        operands.append(cache_k)
        out_specs.append(pl.BlockSpec((blocks_per_step, N_HEADS, HEAD_DIM), lambda i, j, *_: (i * n_tiles + j, 0, 0)))
        out_shape.append(jax.ShapeDtypeStruct((n_steps * blocks_per_step, N_HEADS, HEAD_DIM), _F32))
        scratch += [pltpu.VMEM((2, pages_per_step, PAGE_SIZE, N_HEADS, HEAD_DIM), _F32),
                    pltpu.SemaphoreType.DMA((2,))]
    grid_spec = pltpu.PrefetchScalarGridSpec(
        num_scalar_prefetch=3, grid=(m // tm, n_tiles),
        in_specs=in_specs, out_specs=out_specs, scratch_shapes=scratch)
    outs = pl.pallas_call(
        functools.partial(_proj_kernel, tn=tn, grouped=normed is not None,
                          first_page=first_page, pages_per_step=pages_per_step),
        grid_spec=grid_spec,
        out_shape=out_shape,
        compiler_params=pltpu.CompilerParams(
            dimension_semantics=("arbitrary", "arbitrary"), vmem_limit_bytes=VMEM_LIMIT),
        name="proj_f32" if normed is not None else "proj_bf16",
    )(jnp.asarray(wcol), jnp.asarray(flags), pages, *operands)
    return outs if stream is not None else outs[0]


def _proj_steps(m, tm, tn):
    per_section = (m // tm) * (SECTION // tn)
    return per_section * len(W_SECTIONS_F32), per_section * len(W_SECTIONS_BF16)


def _project_both(x2d, norm_w, w_bf16, epi_w, *, tm, tn, stream=None):
    common = dict(tm=tm, tn=tn)
    f32_args = dict(sections=W_SECTIONS_F32, normed=NORMED_F32, out_dtype=_F32, **common)
    bf16_args = dict(sections=W_SECTIONS_BF16, normed=None, out_dtype=_BF16, **common)
    if stream is None:
        return _project(x2d, norm_w, w_bf16, epi_w, **f32_args), _project(x2d, norm_w, w_bf16, epi_w, **bf16_args)
    pages, cache_k, per_f, per_b = stream
    steps_f, _ = _proj_steps(x2d.shape[0], tm, tn)
    means = []
    pf = _project(x2d, norm_w, w_bf16, epi_w, stream=(pages, cache_k, 0, per_f) if per_f else None, **f32_args)
    if per_f:
        pf, km = pf
        means.append(km)
    pb = _project(x2d, norm_w, w_bf16, epi_w, stream=(pages, cache_k, per_f * steps_f, per_b) if per_b else None,
                  **bf16_args)
    if per_b:
        pb, km = pb
        means.append(km)
    return pf, pb, means


def _plan_page_stream(n_pages, steps_f, steps_b, moba_iters, moba_per_iter):
    if moba_iters * moba_per_iter > n_pages:
        moba_per_iter = 0
    rest = n_pages - moba_iters * moba_per_iter
    best = None
    for per_f in range(0, rest // steps_f + 1, PAGES_PER_BLOCK):
        per_b, rem = divmod(rest - per_f * steps_f, steps_b)
        if rem == 0 and per_b % PAGES_PER_BLOCK == 0 and (best is None or max(per_f, per_b) < max(best)):
            best = (per_f, per_b)
    assert best is not None, "page count does not split over the projection / attention steps"
    return best + (moba_per_iter,)


def _gmlp_kernel(u_ref, v_ref, za_ref, ws_ref, bs_ref, ya_ref, *, n_chunks):
    row = lax.broadcasted_iota(jnp.int32, (CHUNK, CHUNK), 0)
    col = lax.broadcasted_iota(jnp.int32, (CHUNK, CHUNK), 1)
    tril = row >= col
    for g in range(A_GROUPS):
        w_g = jnp.where(tril, ws_ref[g], 0.0).astype(_BF16)
        b_g = bs_ref[:, g:g + 1]
        cs = slice(g * GROUP_DIM, (g + 1) * GROUP_DIM)
        for c in range(n_chunks):
            rs = slice(c * CHUNK, (c + 1) * CHUNK)
            mixed = jnp.dot(w_g, v_ref[rs, cs].astype(_BF16), preferred_element_type=_F32) + b_g
            y = u_ref[rs, cs].astype(_F32) * mixed * _silu(za_ref[rs, cs].astype(_F32))
            ya_ref[rs, cs] = y.astype(ya_ref.dtype)


def _gmlp(pf, pb, w_spatial, b_spatial_t, *, tm):
    m = pf.shape[0]

    def section(c):
        return pl.BlockSpec((tm, D_A), lambda i: (i, c))

    row_spec = section(0)
    return pl.pallas_call(
        functools.partial(_gmlp_kernel, n_chunks=tm // CHUNK),
        grid=(m // tm,),
        in_specs=[section(P_U), section(F_V), section(P_ZA),
                  pl.BlockSpec((A_GROUPS, CHUNK, CHUNK), lambda i: (0, 0, 0)),
                  pl.BlockSpec((CHUNK, A_GROUPS), lambda i: (0, 0))],
        out_specs=row_spec,
        out_shape=jax.ShapeDtypeStruct((m, D_A), _BF16),
        compiler_params=pltpu.CompilerParams(
            dimension_semantics=("arbitrary",), vmem_limit_bytes=VMEM_LIMIT),
        name="gmlp",
    )(pb, pf, pb, w_spatial, b_spatial_t)


def _gmlp_dec_kernel(u_ref, v_ref, za_ref, wrow_ref, brow_ref, ya_ref, *, n_new):
    for t in range(n_new):
        mixed = brow_ref[t:t + 1, :]
        for s in range(t + 1):
            mixed = mixed + wrow_ref[t * n_new + s:t * n_new + s + 1, :] * v_ref[:, s, :]
        y = u_ref[:, t, :].astype(_F32) * mixed * _silu(za_ref[:, t, :].astype(_F32))
        ya_ref[:, t, :] = y.astype(ya_ref.dtype)


def _gmlp_dec(u, v, z_a, wrow, brow):
    n_bat, n_new, _ = v.shape
    full3 = pl.BlockSpec((n_bat, n_new, D_A), lambda i: (0, 0, 0))
    return pl.pallas_call(
        functools.partial(_gmlp_dec_kernel, n_new=n_new),
        grid=(1,),
        in_specs=[full3, full3, full3,
                  pl.BlockSpec(wrow.shape, lambda i: (0, 0)),
                  pl.BlockSpec(brow.shape, lambda i: (0, 0))],
        out_specs=full3,
        out_shape=jax.ShapeDtypeStruct((n_bat, n_new, D_A), _BF16),
        name="gmlp_dec",
    )(u, v, z_a, wrow, brow)


def _topk_mask(gate, col, n_valid, n_cols):
    g = jnp.where(col < n_valid, gate, NEG)
    sel = jnp.zeros(gate.shape, jnp.bool_)
    for j in range(MOBA_TOPK):
        m = jnp.max(g, axis=-1, keepdims=True)
        idx = jnp.min(jnp.where(g == m, col, n_cols), axis=-1, keepdims=True)
        pick = col == idx
        sel = sel | (pick & (j < n_valid))
        g = jnp.where(pick, REMOVED, g)
    return sel


def _bf16_pieces(x, n):
    out = []
    for _ in range(n):
        piece = float(np.asarray(x, np.float32).astype(_BF16).astype(np.float32))
        out.append(piece)
        x = x - piece
    return tuple(out)


_LOG2E = 1.4426950408889634
_LOG2E_PIECES = _bf16_pieces(_LOG2E, 3)
_N_PIECES = len(_LOG2E_PIECES)


def _lane_values(lane, first, values):
    out = jnp.zeros(lane.shape, _F32)
    for p, val in enumerate(values):
        out = jnp.where(lane == first + p, val, out)
    return out


def _moba_kernel(pages_ref, slopes_ref, q_ref, k_ref, v_ref, zb_ref, *rest,
                 n_blocks, sub, n_q, first_page, pages_per_iter):
    if pages_per_iter:
        (ck_ref, o_ref, kmo_ref, kaug_ref, vb_ref, km_ref, qa_ref, m_ref, l_ref, acc_ref, sa_ref, sb_ref,
         pbuf, sems) = rest
    else:
        o_ref, kaug_ref, vb_ref, km_ref, qa_ref, m_ref, l_ref, acc_ref, sa_ref, sb_ref = rest
    h = pl.program_id(0)
    qi = pl.program_id(1)
    blk = MOBA_BLOCK
    lane_hi = n_blocks
    lane_lo = lane_hi + _N_PIECES
    lane_t0 = lane_lo + _N_PIECES
    slope = slopes_ref[h]
    coef = [slope * c for c in _LOG2E_PIECES]
    lane = lax.broadcasted_iota(jnp.int32, (blk, HEAD_DIM), 1)
    hi_lanes = (lane >= lane_hi) & (lane < lane_lo)

    @pl.when(qi == 0)
    def _():
        km_ref[...] = jnp.zeros(km_ref.shape, _F32)
        row_f = lax.broadcasted_iota(jnp.int32, (blk, HEAD_DIM), 0).astype(_F32)
        fixed = (_lane_values(lane, lane_t0, [-c * blk for c in coef])
                 + jnp.where((lane >= lane_lo) & (lane < lane_t0), row_f, 0.0))

        def body(n, carry):
            rows = pl.ds(pl.multiple_of(n * blk, blk), blk)
            kf = k_ref[rows, :]
            aug = fixed + jnp.where(lane == n, NEG, 0.0) + jnp.where(hi_lanes, jnp.asarray(n * blk, _F32), 0.0)
            kaug_ref[rows, :HEAD_DIM] = kf.astype(_BF16)
            kaug_ref[rows, HEAD_DIM:] = aug.astype(_BF16)
            vb_ref[rows, :] = v_ref[rows, :].astype(_BF16)
            km_ref[pl.ds(n, 1), :] = jnp.mean(kf, axis=0, keepdims=True)
            return carry
        lax.fori_loop(0, n_blocks, body, 0)

    q_consts = (_lane_values(lane, lane_hi, coef) + _lane_values(lane, lane_lo, coef)
                + jnp.where((lane >= lane_t0) & (lane < lane_t0 + _N_PIECES), jnp.asarray(qi * sub, _F32), 0.0))
    for r in range(sub):
        rs = slice(r * blk, (r + 1) * blk)
        qf = q_ref[rs, :]
        gate = lax.dot_general(qf, km_ref[...], _NT, precision=lax.Precision.HIGHEST,
                               preferred_element_type=_F32)
        own = qi * sub + r
        open_blocks = _topk_mask(gate, lane, own, HEAD_DIM) | (lane == own)
        qa_ref[rs, :HEAD_DIM] = (qf * (SCALE * _LOG2E)).astype(_BF16)
        qa_ref[rs, HEAD_DIM:] = jnp.where(lane < n_blocks, jnp.where(open_blocks, 0.0, 1.0), q_consts).astype(_BF16)

    def scores(r, k_tile):
        rs = slice(r * blk, (r + 1) * blk)
        return lax.dot_general(qa_ref[rs, :], k_tile, _NT, preferred_element_type=_F32)

    def attend(r, s, v_tile, first):
        rs = slice(r * blk, (r + 1) * blk)
        s0, s1 = s[:, :HEAD_DIM], s[:, HEAD_DIM:]
        mx = jnp.max(jnp.maximum(s0, s1), axis=-1, keepdims=True)
        if first:
            m_new = jnp.broadcast_to(mx, (blk, HEAD_DIM))
        else:
            m_old = m_ref[rs, :]
            m_new = jnp.maximum(m_old, mx)
        p0 = jnp.exp2(s0 - m_new)
        p1 = jnp.exp2(s1 - m_new)
        row_sum = jnp.sum(p0 + p1, axis=-1, keepdims=True)
        pv = jnp.dot(jnp.concatenate([p0, p1], axis=-1).astype(_BF16), v_tile, preferred_element_type=_F32)
        if first:
            l_ref[rs, :] = jnp.broadcast_to(row_sum, (blk, HEAD_DIM))
            acc_ref[rs, :] = pv
        else:
            alpha = jnp.exp2(m_old - m_new)
            l_ref[rs, :] = alpha * l_ref[rs, :] + row_sum
            acc_ref[rs, :] = alpha * acc_ref[rs, :] + pv
        m_ref[rs, :] = m_new

    ri = lax.broadcasted_iota(jnp.int32, (blk, blk), 0)
    ci = lax.broadcasted_iota(jnp.int32, (blk, blk), 1)
    causal = ri >= ci
    for r in range(sub):
        for r2 in [r] + list(range(r)):
            rows = pl.ds(pl.multiple_of((qi * sub + r2) * blk, blk), blk)
            s = scores(r, kaug_ref[rows, :])
            attend(r, jnp.where(causal, s, NEG) if r2 == r else s, vb_ref[rows, :], first=(r2 == r))

    n_past = qi * sub

    def block_rows(n):
        return pl.ds(pl.multiple_of(n * blk, blk), blk)

    def produce(n, dst_ref):
        k_tile = kaug_ref[block_rows(n), :]
        for r in range(sub):
            dst_ref[r * blk:(r + 1) * blk, :] = scores(r, k_tile)

    def consume(n, src_ref):
        v_tile = vb_ref[block_rows(n), :]
        for r in range(sub):
            attend(r, src_ref[r * blk:(r + 1) * blk, :], v_tile, first=False)

    @pl.when(n_past > 0)
    def _():
        produce(0, sa_ref)

    iters_per_head = (sub // 2) * (n_q * (n_q - 1) // 2)
    n_iters = N_HEADS * iters_per_head
    it_base = h * iters_per_head + (sub // 2) * ((qi * (qi - 1)) // 2)

    def chunk_copies(it, sl):
        base = first_page + it * pages_per_iter
        return [pltpu.make_async_copy(ck_ref.at[pages_ref[base + r]], pbuf.at[sl, r], sems.at[sl])
                for r in range(pages_per_iter)]

    if pages_per_iter:
        @pl.when((h == 0) & (qi == 0))
        def _():
            for cp in chunk_copies(0, 0):
                cp.start()

    def body(j, carry):
        if pages_per_iter:
            it = it_base + j
            slot = it % 2
            for cp in chunk_copies(jnp.minimum(it + 1, n_iters - 1), 1 - slot):
                cp.start()
            for cp in chunk_copies(it, slot):
                cp.wait()
        n0 = 2 * j
        produce(n0 + 1, sb_ref)
        consume(n0, sa_ref)
        produce(jnp.minimum(n0 + 2, n_past - 1), sa_ref)
        consume(n0 + 1, sb_ref)
        blocks_per_iter = pages_per_iter // PAGES_PER_BLOCK
        for b in range(blocks_per_iter):
            tot = jnp.sum(pbuf[slot, PAGES_PER_BLOCK * b], axis=0)
            for r in range(1, PAGES_PER_BLOCK):
                tot = tot + jnp.sum(pbuf[slot, PAGES_PER_BLOCK * b + r], axis=0)
            kmo_ref[it * blocks_per_iter + b] = tot * (1.0 / MOBA_BLOCK)
        return carry
    lax.fori_loop(0, n_past // 2, body, 0)

    if pages_per_iter:
        @pl.when((h == pl.num_programs(0) - 1) & (qi == n_q - 1))
        def _():
            for cp in chunk_copies(n_iters - 1, 1 - (n_iters - 1) % 2):
                cp.wait()

    o = acc_ref[...] / l_ref[...]
    o_ref[...] = (o * _silu(zb_ref[...].astype(_F32))).astype(o_ref.dtype)


def _moba_iterations(n_pos, sub):
    n_q = n_pos // (sub * MOBA_BLOCK)
    return N_HEADS * (sub // 2) * (n_q * (n_q - 1) // 2)


def _moba(pf, pb, slopes, *, sub, stream=None):
    n_pos = pf.shape[0]
    n_blocks = n_pos // MOBA_BLOCK
    tq = sub * MOBA_BLOCK
    n_q = n_pos // tq
    assert n_pos % tq == 0 and sub % 2 == 0 and n_blocks + 3 * _N_PIECES <= HEAD_DIM

    def tile_of(section):
        return pl.BlockSpec((tq, HEAD_DIM), lambda h, i, *_: (i, section * N_HEADS + h))

    def head_all(section):
        return pl.BlockSpec((n_pos, HEAD_DIM), lambda h, i, *_: (0, section * N_HEADS + h))

    in_specs = [pl.BlockSpec(memory_space=pltpu.SMEM), tile_of(F_Q), head_all(F_K), head_all(F_VB), tile_of(P_ZB)]
    operands = [slopes, pf, pf, pf, pb]
    out_specs = [tile_of(0)]
    out_shape = [jax.ShapeDtypeStruct((n_pos, D_B), _BF16)]
    scratch = [pltpu.VMEM((n_pos, 2 * HEAD_DIM), _BF16),
               pltpu.VMEM((n_pos, HEAD_DIM), _BF16),
               pltpu.VMEM((HEAD_DIM, HEAD_DIM), _F32),
               pltpu.VMEM((tq, 2 * HEAD_DIM), _BF16),
               pltpu.VMEM((tq, HEAD_DIM), _F32),
               pltpu.VMEM((tq, HEAD_DIM), _F32),
               pltpu.VMEM((tq, HEAD_DIM), _F32),
               pltpu.VMEM((tq, MOBA_BLOCK), _F32),
               pltpu.VMEM((tq, MOBA_BLOCK), _F32)]
    if stream is None:
        pages, first_page, pages_per_iter = jnp.zeros((1,), jnp.int32), 0, 0
    else:
        pages, cache_k, first_page, pages_per_iter = stream
        n_iters = _moba_iterations(n_pos, sub)
        assert n_iters > 0 and pages_per_iter % PAGES_PER_BLOCK == 0 and first_page % PAGES_PER_BLOCK == 0
        n_km = n_iters * pages_per_iter // PAGES_PER_BLOCK
        in_specs.append(pl.BlockSpec(memory_space=pl.ANY))
        operands.append(cache_k)
        out_specs.append(pl.BlockSpec((n_km, N_HEADS, HEAD_DIM), lambda h, i, *_: (0, 0, 0)))
        out_shape.append(jax.ShapeDtypeStruct((n_km, N_HEADS, HEAD_DIM), _F32))
        scratch += [pltpu.VMEM((2, pages_per_iter, PAGE_SIZE, N_HEADS, HEAD_DIM), _F32),
                    pltpu.SemaphoreType.DMA((2,))]
    grid_spec = pltpu.PrefetchScalarGridSpec(
        num_scalar_prefetch=1, grid=(N_HEADS, n_q),
        in_specs=in_specs, out_specs=out_specs, scratch_shapes=scratch)
    outs = pl.pallas_call(
        functools.partial(_moba_kernel, n_blocks=n_blocks, sub=sub, n_q=n_q,
                          first_page=first_page, pages_per_iter=pages_per_iter),
        grid_spec=grid_spec,
        out_shape=out_shape,
        compiler_params=pltpu.CompilerParams(
            dimension_semantics=("arbitrary", "arbitrary"), vmem_limit_bytes=VMEM_LIMIT),
        name="moba",
    )(pages, *operands)
    return outs if stream is not None else outs[0]


def _gate_kernel(q_ref, km_ref, idx_ref, *, n_full, n_new):
    col = lax.broadcasted_iota(jnp.int32, (n_new, n_full), 1)
    lane = lax.broadcasted_iota(jnp.int32, (n_new, 128), 1)
    for h in range(N_HEADS):
        qh = q_ref[0, :, h * HEAD_DIM:(h + 1) * HEAD_DIM]
        g = lax.dot_general(qh, km_ref[0, :, h, :], _NT, precision=lax.Precision.HIGHEST,
                            preferred_element_type=_F32)
        tile = jnp.zeros((n_new, 128), jnp.int32)
        for j in range(MOBA_TOPK):
            m = jnp.max(g, axis=-1, keepdims=True)
            idx = jnp.min(jnp.where(g == m, col, n_full), axis=-1, keepdims=True)
            tile = jnp.where(lane == j, idx, tile)
            g = jnp.where(col == idx, REMOVED, g)
        idx_ref[0, h * n_new:(h + 1) * n_new, :] = tile


def _gate_topk(q3, k_mean):
    n_bat, n_new, _ = q3.shape
    n_full = k_mean.shape[1]
    return pl.pallas_call(
        functools.partial(_gate_kernel, n_full=n_full, n_new=n_new),
        grid=(n_bat,),
        in_specs=[pl.BlockSpec((1, n_new, D_B), lambda b: (b, 0, 0)),
                  pl.BlockSpec((1, n_full, N_HEADS, HEAD_DIM), lambda b: (b, 0, 0, 0))],
        out_specs=pl.BlockSpec((1, N_HEADS * n_new, 128), lambda b: (b, 0, 0)),
        out_shape=jax.ShapeDtypeStruct((n_bat, N_HEADS * n_new, 128), jnp.int32),
        name="gate",
    )(q3, k_mean)


def _moba_dec_kernel(pt_ref, idx_ref, slopes_ref, q_ref, kn_ref, vn_ref, zb_ref, ck_ref, cv_ref,
                     o_ref, kbuf, vbuf, sems, *, n_new, past):
    per_t = MOBA_TOPK * PAGES_PER_BLOCK
    n_flat = MOBA_TOPK * MOBA_BLOCK
    b = pl.program_id(0)
    h = pl.program_id(1)
    n_heads = pl.num_programs(1)
    step = b * n_heads + h
    n_steps = pl.num_programs(0) * n_heads
    n_slots = kbuf.shape[0]
    ahead = n_slots - 1
    slot = step % n_slots

    def page_copies(st, sl):
        bb, hh = st // n_heads, st % n_heads
        copies = []
        for t in range(n_new):
            for c in range(per_t):
                blk_idx = idx_ref[bb, (hh * n_new + t) * MOBA_TOPK + c // PAGES_PER_BLOCK]
                page = pt_ref[bb, blk_idx * PAGES_PER_BLOCK + c % PAGES_PER_BLOCK]
                copies.append(pltpu.make_async_copy(
                    ck_ref.at[page, :, hh, :], kbuf.at[sl, t * per_t + c], sems.at[sl]))
                copies.append(pltpu.make_async_copy(
                    cv_ref.at[page, :, hh, :], vbuf.at[sl, t * per_t + c], sems.at[sl]))
        return copies

    @pl.when(step == 0)
    def _():
        for st in range(ahead):
            @pl.when(st < n_steps)
            def _(st=st):
                for cp in page_copies(st, st % n_slots):
                    cp.start()

    @pl.when(step + ahead < n_steps)
    def _():
        for cp in page_copies(step + ahead, (step + ahead) % n_slots):
            cp.start()

    for cp in page_copies(step, slot):
        cp.wait()

    slope = slopes_ref[h]
    qf = q_ref[0]
    qb = qf.astype(_BF16)
    row = lax.broadcasted_iota(jnp.int32, (n_new, n_flat), 0)
    lane = lax.broadcasted_iota(jnp.int32, (n_new, n_flat), 1)

    s_sel = jnp.zeros((n_new, n_flat), _F32)
    base = jnp.zeros((n_new, n_flat), jnp.int32)
    for t in range(n_new):
        k_cat = kbuf[slot, t * per_t:(t + 1) * per_t].reshape(n_flat, HEAD_DIM).astype(_BF16)
        s_t = lax.dot_general(qb, k_cat, _NT, preferred_element_type=_F32)
        s_sel = jnp.where(row == t, s_t, s_sel)
        for c in range(per_t):
            blk_idx = idx_ref[b, (h * n_new + t) * MOBA_TOPK + c // PAGES_PER_BLOCK]
            start = blk_idx * MOBA_BLOCK + (c % PAGES_PER_BLOCK) * PAGE_SIZE - c * PAGE_SIZE
            in_page = (row == t) & (lane >= c * PAGE_SIZE) & (lane < (c + 1) * PAGE_SIZE)
            base = jnp.where(in_page, start, base)
    dist_sel = (past + row - (base + lane)).astype(_F32)
    s_sel = s_sel * SCALE - slope * dist_sel

    lane_o = lax.broadcasted_iota(jnp.int32, (n_new, 128), 1)
    row_o = lax.broadcasted_iota(jnp.int32, (n_new, 128), 0)
    knf = kn_ref[0]
    s_own = jnp.full((n_new, 128), NEG, _F32)
    for t2 in range(n_new):
        dots = jnp.sum(qf * knf[t2:t2 + 1, :], axis=-1, keepdims=True)
        s_own = jnp.where(lane_o == t2, dots, s_own)
    dist_own = (row_o - lane_o).astype(_F32)
    s_own = jnp.where((lane_o < n_new) & (row_o >= lane_o), s_own * SCALE - slope * dist_own, NEG)

    m = jnp.maximum(jnp.max(s_sel, axis=-1, keepdims=True), jnp.max(s_own, axis=-1, keepdims=True))
    p_sel = jnp.exp(s_sel - m)
    p_own = jnp.exp(s_own - m)
    l = jnp.sum(p_sel, axis=-1, keepdims=True) + jnp.sum(p_own, axis=-1, keepdims=True)
    p_sel = (p_sel / l).astype(_BF16)
    p_own = p_own / l

    row_d = lax.broadcasted_iota(jnp.int32, (n_new, HEAD_DIM), 0)
    o = jnp.zeros((n_new, HEAD_DIM), _F32)
    for t in range(n_new):
        v_cat = vbuf[slot, t * per_t:(t + 1) * per_t].reshape(n_flat, HEAD_DIM).astype(_BF16)
        o_t = jnp.dot(p_sel, v_cat, preferred_element_type=_F32)
        o = jnp.where(row_d == t, o_t, o)
    vnf = vn_ref[0]
    for t2 in range(n_new):
        w = jnp.sum(jnp.where(lane_o == t2, p_own, 0.0), axis=-1, keepdims=True)
        o = o + w * vnf[t2:t2 + 1, :]
    o_ref[0] = (o * _silu(zb_ref[0].astype(_F32))).astype(o_ref.dtype)


def _moba_dec(page_table, idx, slopes, q3, k3, v3, zb3, cache_k, cache_v, *, past):
    n_bat, n_new, _ = q3.shape
    n_pg = n_new * MOBA_TOPK * PAGES_PER_BLOCK
    tok = pl.BlockSpec((1, n_new, HEAD_DIM), lambda b, h, pt, ix: (b, 0, h))
    hbm = pl.BlockSpec(memory_space=pl.ANY)
    grid_spec = pltpu.PrefetchScalarGridSpec(
        num_scalar_prefetch=2,
        grid=(n_bat, N_HEADS),
        in_specs=[pl.BlockSpec(memory_space=pltpu.SMEM), tok, tok, tok, tok, hbm, hbm],
        out_specs=tok,
        scratch_shapes=[pltpu.VMEM((DEC_SLOTS, n_pg, PAGE_SIZE, HEAD_DIM), _F32),
                        pltpu.VMEM((DEC_SLOTS, n_pg, PAGE_SIZE, HEAD_DIM), _F32),
                        pltpu.SemaphoreType.DMA((DEC_SLOTS,))],
    )
    return pl.pallas_call(
        functools.partial(_moba_dec_kernel, n_new=n_new, past=past),
        grid_spec=grid_spec,
        out_shape=jax.ShapeDtypeStruct((n_bat, n_new, D_B), _BF16),
        compiler_params=pltpu.CompilerParams(
            dimension_semantics=("arbitrary", "arbitrary"), vmem_limit_bytes=VMEM_LIMIT),
        name="moba_dec",
    )(page_table, idx, slopes, q3, k3, v3, zb3, cache_k, cache_v)


def _merge_kernel(x_ref, ya_ref, yb_ref, ga_ref, gb_ref, woa_ref, wob_ref, wo_ref, y_ref):
    br_a = jnp.dot(ya_ref[...], woa_ref[...], preferred_element_type=_F32)
    br_b = jnp.dot(yb_ref[...], wob_ref[...], preferred_element_type=_F32)
    mix = (jax.nn.sigmoid(ga_ref[...].astype(_F32)) * br_a
           + jax.nn.sigmoid(gb_ref[...].astype(_F32)) * br_b)
    y_ref[...] = x_ref[...] + jnp.dot(mix.astype(_BF16), wo_ref[...], preferred_element_type=_F32)


def _merge(x2d, y_a, y_b, pb, w_oa, w_ob, w_o, *, tm):
    m = x2d.shape[0]

    def rows(width, col=0):
        return pl.BlockSpec((tm, width), lambda i: (i, col))

    def whole(shape):
        return pl.BlockSpec(shape, lambda i: (0, 0), pipeline_mode=pl.Buffered(1))

    return pl.pallas_call(
        _merge_kernel,
        grid=(m // tm,),
        in_specs=[rows(D_MODEL), rows(D_A), rows(D_B),
                  rows(D_MODEL, P_GA * SECTION // D_MODEL), rows(D_MODEL, P_GB * SECTION // D_MODEL),
                  whole(w_oa.shape), whole(w_ob.shape), whole(w_o.shape)],
        out_specs=rows(D_MODEL),
        out_shape=jax.ShapeDtypeStruct((m, D_MODEL), _F32),
        compiler_params=pltpu.CompilerParams(
            dimension_semantics=("arbitrary",), vmem_limit_bytes=VMEM_LIMIT),
        name="merge",
    )(x2d, y_a, y_b, pb, pb, w_oa, w_ob, w_o)


def kernel(x_prompt, x_sample, cache_k, cache_v, page_table, norm_w, w_in, v_norm_w, q_norm_w,
           k_norm_w, w_spatial, b_spatial, w_out_a, w_out_b, w_out):
    depth = norm_w.shape[0]
    assert depth == 1, "single-layer trunk"
    n_bp, seq, _ = x_prompt.shape
    assert n_bp == 1 and seq % MOBA_BLOCK == 0
    n_bat, n_new, _ = x_sample.shape
    n_pages = page_table.shape[1]
    past = n_pages * PAGE_SIZE
    assert past % MOBA_BLOCK == 0 and n_pages // PAGES_PER_BLOCK >= MOBA_TOPK and n_new <= CHUNK
    layer = 0

    slopes = jnp.asarray(np.array([2.0 ** (-8.0 * (h + 1) / N_HEADS) for h in range(N_HEADS)], np.float32))
    ones = jnp.ones((SECTION,), _F32)
    epi_w = jnp.concatenate([
        ones, v_norm_w[layer], ones, jnp.tile(q_norm_w[layer], N_HEADS), jnp.tile(k_norm_w[layer], N_HEADS),
        ones, ones, ones, ones, ones, ones]).reshape(1, D_IN)
    nw = norm_w[layer].reshape(1, D_MODEL)
    w_in_b = w_in[layer].astype(_BF16)
    w_oa_b = w_out_a[layer].astype(_BF16)
    w_ob_b = w_out_b[layer].astype(_BF16)
    w_o_b = w_out[layer].astype(_BF16)

    xp = x_prompt.reshape(seq, D_MODEL)
    tm_p, tn_p, sub = 1024, 512, 4
    pages = page_table.reshape(-1)
    steps_f, steps_b = _proj_steps(seq, tm_p, tn_p)
    moba_iters = _moba_iterations(seq, sub)
    per_f, per_b, per_it = _plan_page_stream(pages.shape[0], steps_f, steps_b, moba_iters, MOBA_PAGES_PER_ITER)
    pf, pb, means = _project_both(xp, nw, w_in_b, epi_w, tm=tm_p, tn=tn_p,
                                  stream=(pages, cache_k[layer], per_f, per_b))
    y_a = _gmlp(pf, pb, w_spatial[layer], b_spatial[layer].T, tm=512)
    if per_it:
        y_b, km = _moba(pf, pb, slopes, sub=sub,
                        stream=(pages, cache_k[layer], per_f * steps_f + per_b * steps_b, per_it))
        means.append(km)
    else:
        y_b = _moba(pf, pb, slopes, sub=sub)
    k_mean = jnp.concatenate(means, axis=0)
    y_prompt = _merge(xp, y_a, y_b, pb, w_oa_b, w_ob_b, w_o_b, tm=256)

    def section(arr, c):
        return arr[:, c * SECTION:(c + 1) * SECTION]

    k, vb = section(pf, F_K), section(pf, F_VB)

    m_s = n_bat * n_new
    xs = x_sample.reshape(m_s, D_MODEL)
    pf_s, pb_s = _project_both(xs, nw, w_in_b, epi_w, tm=m_s, tn=512)
    v_s, q_s, k_s, vb_s = (section(pf_s, c) for c in (F_V, F_Q, F_K, F_VB))
    u_s, za_s, zb_s = (section(pb_s, c) for c in (P_U, P_ZA, P_ZB))
    w4 = w_spatial[layer][:, :n_new, :n_new]
    wrow = jnp.repeat(w4.transpose(1, 2, 0), GROUP_DIM, axis=-1).reshape(n_new * n_new, D_A)
    brow = jnp.repeat(b_spatial[layer][:, :n_new].T, GROUP_DIM, axis=-1)
    three = (n_bat, n_new, D_A)
    ya_s = _gmlp_dec(u_s.reshape(three), v_s.reshape(three), za_s.reshape(three), wrow, brow)
    q3 = q_s.reshape(n_bat, n_new, D_B)
    n_full = n_pages // PAGES_PER_BLOCK
    idx_pad = _gate_topk(q3, k_mean.reshape(n_bat, n_full, N_HEADS, HEAD_DIM))
    idx = idx_pad[:, :, :MOBA_TOPK].reshape(n_bat, N_HEADS * n_new * MOBA_TOPK)
    yb_s = _moba_dec(page_table, idx, slopes, q3, k_s.reshape(n_bat, n_new, D_B),
                     vb_s.reshape(n_bat, n_new, D_B), zb_s.reshape(n_bat, n_new, D_B),
                     cache_k[layer], cache_v[layer], past=past)
    y_sample = _merge(xs, ya_s.reshape(m_s, D_A), yb_s.reshape(m_s, D_B), pb_s,
                      w_oa_b, w_ob_b, w_o_b, tm=m_s)

    return (y_prompt.reshape(x_prompt.shape),
            y_sample.reshape(x_sample.shape),
            k.reshape(depth, n_bp, seq, N_HEADS, HEAD_DIM),
            vb.reshape(depth, n_bp, seq, N_HEADS, HEAD_DIM),
            k_s.reshape(depth, n_bat, n_new, N_HEADS, HEAD_DIM),
            vb_s.reshape(depth, n_bat, n_new, N_HEADS, HEAD_DIM),
            v_s.reshape(depth, n_bat, n_new, D_A))
```

```python
import functools

import jax
import jax.numpy as jnp
import numpy as np
from jax import lax
from jax.experimental import pallas as pl
from jax.experimental.pallas import tpu as pltpu

D_MODEL = 2048
A_GROUPS = 8
GROUP_DIM = 128
D_A = A_GROUPS * GROUP_DIM
CHUNK = 128
N_HEADS = 8
HEAD_DIM = 128
D_B = N_HEADS * HEAD_DIM
MOBA_BLOCK = 256
MOBA_TOPK = 3
PAGE_SIZE = 128
PAGES_PER_BLOCK = MOBA_BLOCK // PAGE_SIZE
DEC_SLOTS = 3
MOBA_PAGES_PER_ITER = 4
MOBA_PAGES_PER_STEP = 8
MOBA_LOOP_SLOTS = 3
SECTION = 1024
N_SECTIONS = 11
D_IN = N_SECTIONS * SECTION
EPS = 1e-6
NEG = -1e30
REMOVED = -3e38
SCALE = HEAD_DIM ** -0.5
VMEM_LIMIT = 56 * 1024 * 1024

_NT = (((1,), (1,)), ((), ()))
_BF16 = jnp.bfloat16
_F32 = jnp.float32


def _silu(z):
    return z * jax.nn.sigmoid(z)


W_SECTIONS_F32 = (1, 3, 4, 5)
NORMED_F32 = (1, 1, 1, 0)
W_SECTIONS_BF16 = (7, 8, 9, 10, 0, 2, 6)
F_V, F_Q, F_K, F_VB = 0, 1, 2, 3
P_GA, P_GB, P_U, P_ZA, P_ZB = 0, 2, 4, 5, 6


def _proj_kernel(wcol_ref, normed_ref, pages_ref, x_ref, nw_ref, w_ref, ew_ref, *rest,
                 tn, grouped, first_page, pages_per_step):
    if pages_per_step:
        ck_ref, o_ref, km_ref, h_ref, pbuf, sems = rest
    else:
        o_ref, h_ref = rest
    j = pl.program_id(1)

    if pages_per_step:
        n_j = pl.num_programs(1)
        step = pl.program_id(0) * n_j + j
        slot = step % 2

        def page_copies(st, sl):
            base = first_page + st * pages_per_step
            return [pltpu.make_async_copy(ck_ref.at[pages_ref[base + r]], pbuf.at[sl, r], sems.at[sl])
                    for r in range(pages_per_step)]

        @pl.when(step == 0)
        def _():
            for cp in page_copies(step, slot):
                cp.start()

        for cp in page_copies(step, slot):
            cp.wait()

        @pl.when(step + 1 < pl.num_programs(0) * n_j)
        def _():
            for cp in page_copies(step + 1, 1 - slot):
                cp.start()

    @pl.when(j == 0)
    def _():
        xf = x_ref[...]
        ms = jnp.mean(xf * xf, axis=-1, keepdims=True)
        h_ref[...] = (xf * lax.rsqrt(ms + EPS) * nw_ref[...]).astype(_BF16)

    acc = jnp.dot(h_ref[...], w_ref[...], preferred_element_type=_F32)
    if grouped:
        use_norm = normed_ref[j] != 0
        for c in range(tn // GROUP_DIM):
            cs = slice(c * GROUP_DIM, (c + 1) * GROUP_DIM)
            blk = acc[:, cs]
            ms = jnp.mean(blk * blk, axis=-1, keepdims=True)
            scale = jnp.where(use_norm, lax.rsqrt(ms + EPS), 1.0)
            o_ref[:, cs] = (blk * scale * ew_ref[:, cs]).astype(o_ref.dtype)
    else:
        o_ref[...] = acc.astype(o_ref.dtype)

    for blk_i in range(pages_per_step // PAGES_PER_BLOCK):
        tot = jnp.sum(pbuf[slot, PAGES_PER_BLOCK * blk_i], axis=0)
        for r in range(1, PAGES_PER_BLOCK):
            tot = tot + jnp.sum(pbuf[slot, PAGES_PER_BLOCK * blk_i + r], axis=0)
        km_ref[blk_i] = tot * (1.0 / MOBA_BLOCK)


def _project(x2d, norm_w, w_bf16, epi_w, *, sections, normed, out_dtype, tm, tn, stream=None):
    m = x2d.shape[0]
    tiles = SECTION // tn
    n_tiles = len(sections) * tiles
    n_steps = (m // tm) * n_tiles
    wcol = np.array([s * tiles + t for s in sections for t in range(tiles)], np.int32)
    flags = np.array([f for f in (normed or (0,) * len(sections)) for _ in range(tiles)], np.int32)
    in_specs = [
        pl.BlockSpec((tm, D_MODEL), lambda i, j, *_: (i, 0)),
        pl.BlockSpec((1, D_MODEL), lambda i, j, *_: (0, 0)),
        pl.BlockSpec((D_MODEL, tn), lambda i, j, wc, *_: (0, wc[j])),
        pl.BlockSpec((1, tn), lambda i, j, wc, *_: (0, wc[j])),
    ]
    out_specs = [pl.BlockSpec((tm, tn), lambda i, j, *_: (i, j))]
    out_shape = [jax.ShapeDtypeStruct((m, len(sections) * SECTION), out_dtype)]
    scratch = [pltpu.VMEM((tm, D_MODEL), _BF16)]
    operands = [x2d, norm_w, w_bf16, epi_w]
    if stream is None:
        pages, first_page, pages_per_step = jnp.zeros((1,), jnp.int32), 0, 0
    else:
        pages, cache_k, first_page, pages_per_step = stream
        assert pages_per_step % PAGES_PER_BLOCK == 0 and first_page % PAGES_PER_BLOCK == 0
        blocks_per_step = pages_per_step // PAGES_PER_BLOCK
        in_specs.append(pl.BlockSpec(memory_space=pl.ANY))
        operands.append(cache_k)
        out_specs.append(pl.BlockSpec((blocks_per_step, N_HEADS, HEAD_DIM), lambda i, j, *_: (i * n_tiles + j, 0, 0)))
        out_shape.append(jax.ShapeDtypeStruct((n_steps * blocks_per_step, N_HEADS, HEAD_DIM), _F32))
        scratch += [pltpu.VMEM((2, pages_per_step, PAGE_SIZE, N_HEADS, HEAD_DIM), _F32),
                    pltpu.SemaphoreType.DMA((2,))]
    grid_spec = pltpu.PrefetchScalarGridSpec(
        num_scalar_prefetch=3, grid=(m // tm, n_tiles),
        in_specs=in_specs, out_specs=out_specs, scratch_shapes=scratch)
    outs = pl.pallas_call(
        functools.partial(_proj_kernel, tn=tn, grouped=normed is not None,
                          first_page=first_page, pages_per_step=pages_per_step),
        grid_spec=grid_spec,
        out_shape=out_shape,
        compiler_params=pltpu.CompilerParams(
            dimension_semantics=("arbitrary", "arbitrary"), vmem_limit_bytes=VMEM_LIMIT),
        name="proj_f32" if normed is not None else "proj_bf16",
    )(jnp.asarray(wcol), jnp.asarray(flags), pages, *operands)
    return outs if stream is not None else outs[0]


def _proj_steps(m, tm, tn):
    per_section = (m // tm) * (SECTION // tn)
    return per_section * len(W_SECTIONS_F32), per_section * len(W_SECTIONS_BF16)


def _project_both(x2d, norm_w, w_bf16, epi_w, *, tm, tn, stream=None):
    common = dict(tm=tm, tn=tn)
    f32_args = dict(sections=W_SECTIONS_F32, normed=NORMED_F32, out_dtype=_F32, **common)
    bf16_args = dict(sections=W_SECTIONS_BF16, normed=None, out_dtype=_BF16, **common)
    if stream is None:
        return _project(x2d, norm_w, w_bf16, epi_w, **f32_args), _project(x2d, norm_w, w_bf16, epi_w, **bf16_args)
    pages, cache_k, per_f, per_b = stream
    steps_f, _ = _proj_steps(x2d.shape[0], tm, tn)
    means = []
    pf = _project(x2d, norm_w, w_bf16, epi_w, stream=(pages, cache_k, 0, per_f) if per_f else None, **f32_args)
    if per_f:
        pf, km = pf
        means.append(km)
    pb = _project(x2d, norm_w, w_bf16, epi_w, stream=(pages, cache_k, per_f * steps_f, per_b) if per_b else None,
                  **bf16_args)
    if per_b:
        pb, km = pb
        means.append(km)
    return pf, pb, means


def _plan_page_stream(n_pages, steps_f, steps_b, moba_pages):
    rest = n_pages - moba_pages
    assert rest >= 0
    best = None
    for per_f in range(0, rest // steps_f + 1, PAGES_PER_BLOCK):
        per_b, rem = divmod(rest - per_f * steps_f, steps_b)
        if rem == 0 and per_b % PAGES_PER_BLOCK == 0 and (best is None or max(per_f, per_b) < max(best)):
            best = (per_f, per_b)
    assert best is not None, "page count does not split over the projection / attention steps"
    return best


def _gmlp_kernel(u_ref, v_ref, za_ref, ws_ref, bs_ref, ya_ref, *, n_chunks):
    row = lax.broadcasted_iota(jnp.int32, (CHUNK, CHUNK), 0)
    col = lax.broadcasted_iota(jnp.int32, (CHUNK, CHUNK), 1)
    tril = row >= col
    for g in range(A_GROUPS):
        w_g = jnp.where(tril, ws_ref[g], 0.0).astype(_BF16)
        b_g = bs_ref[:, g:g + 1]
        cs = slice(g * GROUP_DIM, (g + 1) * GROUP_DIM)
        for c in range(n_chunks):
            rs = slice(c * CHUNK, (c + 1) * CHUNK)
            mixed = jnp.dot(w_g, v_ref[rs, cs].astype(_BF16), preferred_element_type=_F32) + b_g
            y = u_ref[rs, cs].astype(_F32) * mixed * _silu(za_ref[rs, cs].astype(_F32))
            ya_ref[rs, cs] = y.astype(ya_ref.dtype)


def _gmlp(pf, pb, w_spatial, b_spatial_t, *, tm):
    m = pf.shape[0]

    def section(c):
        return pl.BlockSpec((tm, D_A), lambda i: (i, c))

    row_spec = section(0)
    return pl.pallas_call(
        functools.partial(_gmlp_kernel, n_chunks=tm // CHUNK),
        grid=(m // tm,),
        in_specs=[section(P_U), section(F_V), section(P_ZA),
                  pl.BlockSpec((A_GROUPS, CHUNK, CHUNK), lambda i: (0, 0, 0)),
                  pl.BlockSpec((CHUNK, A_GROUPS), lambda i: (0, 0))],
        out_specs=row_spec,
        out_shape=jax.ShapeDtypeStruct((m, D_A), _BF16),
        compiler_params=pltpu.CompilerParams(
            dimension_semantics=("arbitrary",), vmem_limit_bytes=VMEM_LIMIT),
        name="gmlp",
    )(pb, pf, pb, w_spatial, b_spatial_t)


def _gmlp_dec_kernel(u_ref, v_ref, za_ref, wrow_ref, brow_ref, ya_ref, *, n_new):
    for t in range(n_new):
        mixed = brow_ref[t:t + 1, :]
        for s in range(t + 1):
            mixed = mixed + wrow_ref[t * n_new + s:t * n_new + s + 1, :] * v_ref[:, s, :]
        y = u_ref[:, t, :].astype(_F32) * mixed * _silu(za_ref[:, t, :].astype(_F32))
        ya_ref[:, t, :] = y.astype(ya_ref.dtype)


def _gmlp_dec(u, v, z_a, wrow, brow):
    n_bat, n_new, _ = v.shape
    full3 = pl.BlockSpec((n_bat, n_new, D_A), lambda i: (0, 0, 0))
    return pl.pallas_call(
        functools.partial(_gmlp_dec_kernel, n_new=n_new),
        grid=(1,),
        in_specs=[full3, full3, full3,
                  pl.BlockSpec(wrow.shape, lambda i: (0, 0)),
                  pl.BlockSpec(brow.shape, lambda i: (0, 0))],
        out_specs=full3,
        out_shape=jax.ShapeDtypeStruct((n_bat, n_new, D_A), _BF16),
        name="gmlp_dec",
    )(u, v, z_a, wrow, brow)


def _topk_mask(gate, col, n_valid, n_cols):
    g = jnp.where(col < n_valid, gate, NEG)
    sel = jnp.zeros(gate.shape, jnp.bool_)
    for j in range(MOBA_TOPK):
        m = jnp.max(g, axis=-1, keepdims=True)
        idx = jnp.min(jnp.where(g == m, col, n_cols), axis=-1, keepdims=True)
        pick = col == idx
        sel = sel | (pick & (j < n_valid))
        g = jnp.where(pick, REMOVED, g)
    return sel


def _bf16_pieces(x, n):
    out = []
    for _ in range(n):
        piece = float(np.asarray(x, np.float32).astype(_BF16).astype(np.float32))
        out.append(piece)
        x = x - piece
    return tuple(out)


_LOG2E = 1.4426950408889634
_LOG2E_PIECES = _bf16_pieces(_LOG2E, 3)
_N_PIECES = len(_LOG2E_PIECES)


def _lane_values(lane, first, values):
    out = jnp.zeros(lane.shape, _F32)
    for p, val in enumerate(values):
        out = jnp.where(lane == first + p, val, out)
    return out


def _moba_kernel(pages_ref, slopes_ref, q_ref, k_ref, v_ref, zb_ref, *rest,
                 n_blocks, sub, n_q, first_page, pages_per_iter, pages_per_step):
    if pages_per_iter:
        (ck_ref, o_ref, kmo_ref, kms_ref, kaug_ref, vb_ref, km_ref, qa_ref, m_ref, l_ref, acc_ref, sa_ref, sb_ref,
         pbuf, sems, pbuf2, sems2) = rest
    else:
        o_ref, kaug_ref, vb_ref, km_ref, qa_ref, m_ref, l_ref, acc_ref, sa_ref, sb_ref = rest
    h = pl.program_id(0)
    qi = pl.program_id(1)
    blk = MOBA_BLOCK

    if pages_per_iter:
        grid_step = h * n_q + qi
        step_slot = grid_step % 2
        first_step_page = first_page + N_HEADS * (sub // 2) * (n_q * (n_q - 1) // 2) * pages_per_iter

        def step_copies(st, sl):
            base = first_step_page + st * pages_per_step
            return [pltpu.make_async_copy(ck_ref.at[pages_ref[base + r]], pbuf2.at[sl, r], sems2.at[sl])
                    for r in range(pages_per_step)]

        @pl.when(grid_step == 0)
        def _():
            for cp in step_copies(grid_step, step_slot):
                cp.start()

        for cp in step_copies(grid_step, step_slot):
            cp.wait()

        @pl.when(grid_step + 1 < pl.num_programs(0) * n_q)
        def _():
            for cp in step_copies(grid_step + 1, 1 - step_slot):
                cp.start()
    lane_hi = n_blocks
    lane_lo = lane_hi + _N_PIECES
    lane_t0 = lane_lo + _N_PIECES
    slope = slopes_ref[h]
    coef = [slope * c for c in _LOG2E_PIECES]
    lane = lax.broadcasted_iota(jnp.int32, (blk, HEAD_DIM), 1)
    hi_lanes = (lane >= lane_hi) & (lane < lane_lo)

    @pl.when(qi == 0)
    def _():
        km_ref[...] = jnp.zeros(km_ref.shape, _F32)
        row_f = lax.broadcasted_iota(jnp.int32, (blk, HEAD_DIM), 0).astype(_F32)
        fixed = (_lane_values(lane, lane_t0, [-c * blk for c in coef])
                 + jnp.where((lane >= lane_lo) & (lane < lane_t0), row_f, 0.0))

        def body(n, carry):
            rows = pl.ds(pl.multiple_of(n * blk, blk), blk)
            kf = k_ref[rows, :]
            aug = fixed + jnp.where(lane == n, NEG, 0.0) + jnp.where(hi_lanes, jnp.asarray(n * blk, _F32), 0.0)
            kaug_ref[rows, :HEAD_DIM] = kf.astype(_BF16)
            kaug_ref[rows, HEAD_DIM:] = aug.astype(_BF16)
            vb_ref[rows, :] = v_ref[rows, :].astype(_BF16)
            km_ref[pl.ds(n, 1), :] = jnp.mean(kf, axis=0, keepdims=True)
            return carry
        lax.fori_loop(0, n_blocks, body, 0)

    q_consts = (_lane_values(lane, lane_hi, coef) + _lane_values(lane, lane_lo, coef)
                + jnp.where((lane >= lane_t0) & (lane < lane_t0 + _N_PIECES), jnp.asarray(qi * sub, _F32), 0.0))
    for r in range(sub):
        rs = slice(r * blk, (r + 1) * blk)
        qf = q_ref[rs, :]
        gate = lax.dot_general(qf, km_ref[...], _NT, precision=lax.Precision.HIGHEST,
                               preferred_element_type=_F32)
        own = qi * sub + r
        open_blocks = _topk_mask(gate, lane, own, HEAD_DIM) | (lane == own)
        qa_ref[rs, :HEAD_DIM] = (qf * (SCALE * _LOG2E)).astype(_BF16)
        qa_ref[rs, HEAD_DIM:] = jnp.where(lane < n_blocks, jnp.where(open_blocks, 0.0, 1.0), q_consts).astype(_BF16)

    def scores(r, k_tile):
        rs = slice(r * blk, (r + 1) * blk)
        return lax.dot_general(qa_ref[rs, :], k_tile, _NT, preferred_element_type=_F32)

    def attend(r, s, v_tile, first):
        rs = slice(r * blk, (r + 1) * blk)
        s0, s1 = s[:, :HEAD_DIM], s[:, HEAD_DIM:]
        mx = jnp.max(jnp.maximum(s0, s1), axis=-1, keepdims=True)
        if first:
            m_new = jnp.broadcast_to(mx, (blk, HEAD_DIM))
        else:
            m_old = m_ref[rs, :]
            m_new = jnp.maximum(m_old, mx)
        p0 = jnp.exp2(s0 - m_new)
        p1 = jnp.exp2(s1 - m_new)
        row_sum = jnp.sum(p0 + p1, axis=-1, keepdims=True)
        pv = jnp.dot(jnp.concatenate([p0, p1], axis=-1).astype(_BF16), v_tile, preferred_element_type=_F32)
        if first:
            l_ref[rs, :] = jnp.broadcast_to(row_sum, (blk, HEAD_DIM))
            acc_ref[rs, :] = pv
        else:
            alpha = jnp.exp2(m_old - m_new)
            l_ref[rs, :] = alpha * l_ref[rs, :] + row_sum
            acc_ref[rs, :] = alpha * acc_ref[rs, :] + pv
        m_ref[rs, :] = m_new

    ri = lax.broadcasted_iota(jnp.int32, (blk, blk), 0)
    ci = lax.broadcasted_iota(jnp.int32, (blk, blk), 1)
    causal = ri >= ci
    for r in range(sub):
        for r2 in [r] + list(range(r)):
            rows = pl.ds(pl.multiple_of((qi * sub + r2) * blk, blk), blk)
            s = scores(r, kaug_ref[rows, :])
            attend(r, jnp.where(causal, s, NEG) if r2 == r else s, vb_ref[rows, :], first=(r2 == r))

    if pages_per_iter:
        for b in range(pages_per_step // PAGES_PER_BLOCK):
            tot = jnp.sum(pbuf2[step_slot, PAGES_PER_BLOCK * b], axis=0)
            for r in range(1, PAGES_PER_BLOCK):
                tot = tot + jnp.sum(pbuf2[step_slot, PAGES_PER_BLOCK * b + r], axis=0)
            kms_ref[b] = tot * (1.0 / MOBA_BLOCK)

    n_past = qi * sub

    def block_rows(n):
        return pl.ds(pl.multiple_of(n * blk, blk), blk)

    def produce(n, dst_ref):
        k_tile = kaug_ref[block_rows(n), :]
        for r in range(sub):
            dst_ref[r * blk:(r + 1) * blk, :] = scores(r, k_tile)

    def consume(n, src_ref):
        v_tile = vb_ref[block_rows(n), :]
        for r in range(sub):
            attend(r, src_ref[r * blk:(r + 1) * blk, :], v_tile, first=False)

    @pl.when(n_past > 0)
    def _():
        produce(0, sa_ref)

    iters_per_head = (sub // 2) * (n_q * (n_q - 1) // 2)
    n_iters = N_HEADS * iters_per_head
    it_base = h * iters_per_head + (sub // 2) * ((qi * (qi - 1)) // 2)
    n_slots = pbuf.shape[0] if pages_per_iter else 1
    ahead = n_slots - 1

    def chunk_copies(it, sl):
        base = first_page + it * pages_per_iter
        return [pltpu.make_async_copy(ck_ref.at[pages_ref[base + r]], pbuf.at[sl, r], sems.at[sl])
                for r in range(pages_per_iter)]

    if pages_per_iter:
        @pl.when((h == 0) & (qi == 0))
        def _():
            for c in range(ahead):
                for cp in chunk_copies(c, c):
                    cp.start()

    def body(j, carry):
        if pages_per_iter:
            it = it_base + j
            slot = it % n_slots
            for cp in chunk_copies(jnp.minimum(it + ahead, n_iters - 1), (it + ahead) % n_slots):
                cp.start()
            for cp in chunk_copies(it, slot):
                cp.wait()
        n0 = 2 * j
        produce(n0 + 1, sb_ref)
        consume(n0, sa_ref)
        produce(jnp.minimum(n0 + 2, n_past - 1), sa_ref)
        consume(n0 + 1, sb_ref)
        blocks_per_iter = pages_per_iter // PAGES_PER_BLOCK
        for b in range(blocks_per_iter):
            tot = jnp.sum(pbuf[slot, PAGES_PER_BLOCK * b], axis=0)
            for r in range(1, PAGES_PER_BLOCK):
                tot = tot + jnp.sum(pbuf[slot, PAGES_PER_BLOCK * b + r], axis=0)
            kmo_ref[it * blocks_per_iter + b] = tot * (1.0 / MOBA_BLOCK)
        return carry
    lax.fori_loop(0, n_past // 2, body, 0)

    if pages_per_iter:
        @pl.when((h == pl.num_programs(0) - 1) & (qi == n_q - 1))
        def _():
            for c in range(ahead):
                for cp in chunk_copies(n_iters - 1, (n_iters + c) % n_slots):
                    cp.wait()

    o = acc_ref[...] / l_ref[...]
    o_ref[...] = (o * _silu(zb_ref[...].astype(_F32))).astype(o_ref.dtype)


def _moba_iterations(n_pos, sub):
    n_q = n_pos // (sub * MOBA_BLOCK)
    return N_HEADS * (sub // 2) * (n_q * (n_q - 1) // 2)


def _moba(pf, pb, slopes, *, sub, stream=None):
    n_pos = pf.shape[0]
    n_blocks = n_pos // MOBA_BLOCK
    tq = sub * MOBA_BLOCK
    n_q = n_pos // tq
    assert n_pos % tq == 0 and sub % 2 == 0 and n_blocks + 3 * _N_PIECES <= HEAD_DIM

    def tile_of(section):
        return pl.BlockSpec((tq, HEAD_DIM), lambda h, i, *_: (i, section * N_HEADS + h))

    def head_all(section):
        return pl.BlockSpec((n_pos, HEAD_DIM), lambda h, i, *_: (0, section * N_HEADS + h))

    in_specs = [pl.BlockSpec(memory_space=pltpu.SMEM), tile_of(F_Q), head_all(F_K), head_all(F_VB), tile_of(P_ZB)]
    operands = [slopes, pf, pf, pf, pb]
    out_specs = [tile_of(0)]
    out_shape = [jax.ShapeDtypeStruct((n_pos, D_B), _BF16)]
    scratch = [pltpu.VMEM((n_pos, 2 * HEAD_DIM), _BF16),
               pltpu.VMEM((n_pos, HEAD_DIM), _BF16),
               pltpu.VMEM((HEAD_DIM, HEAD_DIM), _F32),
               pltpu.VMEM((tq, 2 * HEAD_DIM), _BF16),
               pltpu.VMEM((tq, HEAD_DIM), _F32),
               pltpu.VMEM((tq, HEAD_DIM), _F32),
               pltpu.VMEM((tq, HEAD_DIM), _F32),
               pltpu.VMEM((tq, MOBA_BLOCK), _F32),
               pltpu.VMEM((tq, MOBA_BLOCK), _F32)]
    if stream is None:
        pages, first_page, pages_per_iter, pages_per_step = jnp.zeros((1,), jnp.int32), 0, 0, 0
    else:
        pages, cache_k, first_page, pages_per_iter, pages_per_step = stream
        n_iters = _moba_iterations(n_pos, sub)
        assert n_iters >= MOBA_LOOP_SLOTS and first_page % PAGES_PER_BLOCK == 0
        assert pages_per_iter > 0 and pages_per_iter % PAGES_PER_BLOCK == 0
        assert pages_per_step > 0 and pages_per_step % PAGES_PER_BLOCK == 0
        n_km = n_iters * pages_per_iter // PAGES_PER_BLOCK
        km_per_step = pages_per_step // PAGES_PER_BLOCK
        in_specs.append(pl.BlockSpec(memory_space=pl.ANY))
        operands.append(cache_k)
        out_specs += [pl.BlockSpec((n_km, N_HEADS, HEAD_DIM), lambda h, i, *_: (0, 0, 0)),
                      pl.BlockSpec((km_per_step, N_HEADS, HEAD_DIM), lambda h, i, *_: (h * n_q + i, 0, 0))]
        out_shape += [jax.ShapeDtypeStruct((n_km, N_HEADS, HEAD_DIM), _F32),
                      jax.ShapeDtypeStruct((N_HEADS * n_q * km_per_step, N_HEADS, HEAD_DIM), _F32)]
        scratch += [pltpu.VMEM((MOBA_LOOP_SLOTS, pages_per_iter, PAGE_SIZE, N_HEADS, HEAD_DIM), _F32),
                    pltpu.SemaphoreType.DMA((MOBA_LOOP_SLOTS,)),
                    pltpu.VMEM((2, pages_per_step, PAGE_SIZE, N_HEADS, HEAD_DIM), _F32),
                    pltpu.SemaphoreType.DMA((2,))]
    grid_spec = pltpu.PrefetchScalarGridSpec(
        num_scalar_prefetch=1, grid=(N_HEADS, n_q),
        in_specs=in_specs, out_specs=out_specs, scratch_shapes=scratch)
    outs = pl.pallas_call(
        functools.partial(_moba_kernel, n_blocks=n_blocks, sub=sub, n_q=n_q, first_page=first_page,
                          pages_per_iter=pages_per_iter, pages_per_step=pages_per_step),
        grid_spec=grid_spec,
        out_shape=out_shape,
        compiler_params=pltpu.CompilerParams(
            dimension_semantics=("arbitrary", "arbitrary"), vmem_limit_bytes=VMEM_LIMIT),
        name="moba",
    )(pages, *operands)
    return outs if stream is not None else outs[0]


def _gate_kernel(q_ref, km_ref, idx_ref, *, n_full, n_new):
    col = lax.broadcasted_iota(jnp.int32, (n_new, n_full), 1)
    lane = lax.broadcasted_iota(jnp.int32, (n_new, 128), 1)
    for h in range(N_HEADS):
        qh = q_ref[0, :, h * HEAD_DIM:(h + 1) * HEAD_DIM]
        g = lax.dot_general(qh, km_ref[0, :, h, :], _NT, precision=lax.Precision.HIGHEST,
                            preferred_element_type=_F32)
        tile = jnp.zeros((n_new, 128), jnp.int32)
        for j in range(MOBA_TOPK):
            m = jnp.max(g, axis=-1, keepdims=True)
            idx = jnp.min(jnp.where(g == m, col, n_full), axis=-1, keepdims=True)
            tile = jnp.where(lane == j, idx, tile)
            g = jnp.where(col == idx, REMOVED, g)
        idx_ref[0, h * n_new:(h + 1) * n_new, :] = tile


def _gate_topk(q3, k_mean):
    n_bat, n_new, _ = q3.shape
    n_full = k_mean.shape[1]
    return pl.pallas_call(
        functools.partial(_gate_kernel, n_full=n_full, n_new=n_new),
        grid=(n_bat,),
        in_specs=[pl.BlockSpec((1, n_new, D_B), lambda b: (b, 0, 0)),
                  pl.BlockSpec((1, n_full, N_HEADS, HEAD_DIM), lambda b: (b, 0, 0, 0))],
        out_specs=pl.BlockSpec((1, N_HEADS * n_new, 128), lambda b: (b, 0, 0)),
        out_shape=jax.ShapeDtypeStruct((n_bat, N_HEADS * n_new, 128), jnp.int32),
        name="gate",
    )(q3, k_mean)


def _moba_dec_kernel(pt_ref, idx_ref, slopes_ref, q_ref, kn_ref, vn_ref, zb_ref, ck_ref, cv_ref,
                     o_ref, kbuf, vbuf, sems, *, n_new, past):
    per_t = MOBA_TOPK * PAGES_PER_BLOCK
    n_flat = MOBA_TOPK * MOBA_BLOCK
    b = pl.program_id(0)
    h = pl.program_id(1)
    n_heads = pl.num_programs(1)
    step = b * n_heads + h
    n_steps = pl.num_programs(0) * n_heads
    n_slots = kbuf.shape[0]
    ahead = n_slots - 1
    slot = step % n_slots

    def page_copies(st, sl):
        bb, hh = st // n_heads, st % n_heads
        copies = []
        for t in range(n_new):
            for c in range(per_t):
                blk_idx = idx_ref[bb, (hh * n_new + t) * MOBA_TOPK + c // PAGES_PER_BLOCK]
                page = pt_ref[bb, blk_idx * PAGES_PER_BLOCK + c % PAGES_PER_BLOCK]
                copies.append(pltpu.make_async_copy(
                    ck_ref.at[page, :, hh, :], kbuf.at[sl, t * per_t + c], sems.at[sl]))
                copies.append(pltpu.make_async_copy(
                    cv_ref.at[page, :, hh, :], vbuf.at[sl, t * per_t + c], sems.at[sl]))
        return copies

    @pl.when(step == 0)
    def _():
        for st in range(ahead):
            @pl.when(st < n_steps)
            def _(st=st):
                for cp in page_copies(st, st % n_slots):
                    cp.start()

    @pl.when(step + ahead < n_steps)
    def _():
        for cp in page_copies(step + ahead, (step + ahead) % n_slots):
            cp.start()

    for cp in page_copies(step, slot):
        cp.wait()

    slope = slopes_ref[h]
    qf = q_ref[0]
    qb = qf.astype(_BF16)
    row = lax.broadcasted_iota(jnp.int32, (n_new, n_flat), 0)
    lane = lax.broadcasted_iota(jnp.int32, (n_new, n_flat), 1)

    s_sel = jnp.zeros((n_new, n_flat), _F32)
    base = jnp.zeros((n_new, n_flat), jnp.int32)
    for t in range(n_new):
        k_cat = kbuf[slot, t * per_t:(t + 1) * per_t].reshape(n_flat, HEAD_DIM).astype(_BF16)
        s_t = lax.dot_general(qb, k_cat, _NT, preferred_element_type=_F32)
        s_sel = jnp.where(row == t, s_t, s_sel)
        for c in range(per_t):
            blk_idx = idx_ref[b, (h * n_new + t) * MOBA_TOPK + c // PAGES_PER_BLOCK]
            start = blk_idx * MOBA_BLOCK + (c % PAGES_PER_BLOCK) * PAGE_SIZE - c * PAGE_SIZE
            in_page = (row == t) & (lane >= c * PAGE_SIZE) & (lane < (c + 1) * PAGE_SIZE)
            base = jnp.where(in_page, start, base)
    dist_sel = (past + row - (base + lane)).astype(_F32)
    s_sel = s_sel * SCALE - slope * dist_sel

    lane_o = lax.broadcasted_iota(jnp.int32, (n_new, 128), 1)
    row_o = lax.broadcasted_iota(jnp.int32, (n_new, 128), 0)
    knf = kn_ref[0]
    s_own = jnp.full((n_new, 128), NEG, _F32)
    for t2 in range(n_new):
        dots = jnp.sum(qf * knf[t2:t2 + 1, :], axis=-1, keepdims=True)
        s_own = jnp.where(lane_o == t2, dots, s_own)
    dist_own = (row_o - lane_o).astype(_F32)
    s_own = jnp.where((lane_o < n_new) & (row_o >= lane_o), s_own * SCALE - slope * dist_own, NEG)

    m = jnp.maximum(jnp.max(s_sel, axis=-1, keepdims=True), jnp.max(s_own, axis=-1, keepdims=True))
    p_sel = jnp.exp(s_sel - m)
    p_own = jnp.exp(s_own - m)
    l = jnp.sum(p_sel, axis=-1, keepdims=True) + jnp.sum(p_own, axis=-1, keepdims=True)
    p_sel = (p_sel / l).astype(_BF16)
    p_own = p_own / l

    row_d = lax.broadcasted_iota(jnp.int32, (n_new, HEAD_DIM), 0)
    o = jnp.zeros((n_new, HEAD_DIM), _F32)
    for t in range(n_new):
        v_cat = vbuf[slot, t * per_t:(t + 1) * per_t].reshape(n_flat, HEAD_DIM).astype(_BF16)
        o_t = jnp.dot(p_sel, v_cat, preferred_element_type=_F32)
        o = jnp.where(row_d == t, o_t, o)
    vnf = vn_ref[0]
    for t2 in range(n_new):
        w = jnp.sum(jnp.where(lane_o == t2, p_own, 0.0), axis=-1, keepdims=True)
        o = o + w * vnf[t2:t2 + 1, :]
    o_ref[0] = (o * _silu(zb_ref[0].astype(_F32))).astype(o_ref.dtype)


def _moba_dec(page_table, idx, slopes, q3, k3, v3, zb3, cache_k, cache_v, *, past):
    n_bat, n_new, _ = q3.shape
    n_pg = n_new * MOBA_TOPK * PAGES_PER_BLOCK
    tok = pl.BlockSpec((1, n_new, HEAD_DIM), lambda b, h, pt, ix: (b, 0, h))
    hbm = pl.BlockSpec(memory_space=pl.ANY)
    grid_spec = pltpu.PrefetchScalarGridSpec(
        num_scalar_prefetch=2,
        grid=(n_bat, N_HEADS),
        in_specs=[pl.BlockSpec(memory_space=pltpu.SMEM), tok, tok, tok, tok, hbm, hbm],
        out_specs=tok,
        scratch_shapes=[pltpu.VMEM((DEC_SLOTS, n_pg, PAGE_SIZE, HEAD_DIM), _F32),
                        pltpu.VMEM((DEC_SLOTS, n_pg, PAGE_SIZE, HEAD_DIM), _F32),
                        pltpu.SemaphoreType.DMA((DEC_SLOTS,))],
    )
    return pl.pallas_call(
        functools.partial(_moba_dec_kernel, n_new=n_new, past=past),
        grid_spec=grid_spec,
        out_shape=jax.ShapeDtypeStruct((n_bat, n_new, D_B), _BF16),
        compiler_params=pltpu.CompilerParams(
            dimension_semantics=("arbitrary", "arbitrary"), vmem_limit_bytes=VMEM_LIMIT),
        name="moba_dec",
    )(page_table, idx, slopes, q3, k3, v3, zb3, cache_k, cache_v)


def _merge_kernel(x_ref, ya_ref, yb_ref, ga_ref, gb_ref, woa_ref, wob_ref, wo_ref, y_ref):
    br_a = jnp.dot(ya_ref[...], woa_ref[...], preferred_element_type=_F32)
    br_b = jnp.dot(yb_ref[...], wob_ref[...], preferred_element_type=_F32)
    mix = (jax.nn.sigmoid(ga_ref[...].astype(_F32)) * br_a
           + jax.nn.sigmoid(gb_ref[...].astype(_F32)) * br_b)
    y_ref[...] = x_ref[...] + jnp.dot(mix.astype(_BF16), wo_ref[...], preferred_element_type=_F32)


def _merge(x2d, y_a, y_b, pb, w_oa, w_ob, w_o, *, tm):
    m = x2d.shape[0]

    def rows(width, col=0):
        return pl.BlockSpec((tm, width), lambda i: (i, col))

    def whole(shape):
        return pl.BlockSpec(shape, lambda i: (0, 0), pipeline_mode=pl.Buffered(1))

    return pl.pallas_call(
        _merge_kernel,
        grid=(m // tm,),
        in_specs=[rows(D_MODEL), rows(D_A), rows(D_B),
                  rows(D_MODEL, P_GA * SECTION // D_MODEL), rows(D_MODEL, P_GB * SECTION // D_MODEL),
                  whole(w_oa.shape), whole(w_ob.shape), whole(w_o.shape)],
        out_specs=rows(D_MODEL),
        out_shape=jax.ShapeDtypeStruct((m, D_MODEL), _F32),
        compiler_params=pltpu.CompilerParams(
            dimension_semantics=("arbitrary",), vmem_limit_bytes=VMEM_LIMIT),
        name="merge",
    )(x2d, y_a, y_b, pb, pb, w_oa, w_ob, w_o)


def kernel(x_prompt, x_sample, cache_k, cache_v, page_table, norm_w, w_in, v_norm_w, q_norm_w,
           k_norm_w, w_spatial, b_spatial, w_out_a, w_out_b, w_out):
    depth = norm_w.shape[0]
    assert depth == 1, "single-layer trunk"
    n_bp, seq, _ = x_prompt.shape
    assert n_bp == 1 and seq % MOBA_BLOCK == 0
    n_bat, n_new, _ = x_sample.shape
    n_pages = page_table.shape[1]
    past = n_pages * PAGE_SIZE
    assert past % MOBA_BLOCK == 0 and n_pages // PAGES_PER_BLOCK >= MOBA_TOPK and n_new <= CHUNK
    layer = 0

    slopes = jnp.asarray(np.array([2.0 ** (-8.0 * (h + 1) / N_HEADS) for h in range(N_HEADS)], np.float32))
    ones = jnp.ones((SECTION,), _F32)
    epi_w = jnp.concatenate([
        ones, v_norm_w[layer], ones, jnp.tile(q_norm_w[layer], N_HEADS), jnp.tile(k_norm_w[layer], N_HEADS),
        ones, ones, ones, ones, ones, ones]).reshape(1, D_IN)
    nw = norm_w[layer].reshape(1, D_MODEL)
    w_in_b = w_in[layer].astype(_BF16)
    w_oa_b = w_out_a[layer].astype(_BF16)
    w_ob_b = w_out_b[layer].astype(_BF16)
    w_o_b = w_out[layer].astype(_BF16)

    xp = x_prompt.reshape(seq, D_MODEL)
    tm_p, tn_p, sub = 1024, 512, 4
    pages = page_table.reshape(-1)
    steps_f, steps_b = _proj_steps(seq, tm_p, tn_p)
    moba_pages = (_moba_iterations(seq, sub) * MOBA_PAGES_PER_ITER
                  + N_HEADS * (seq // (sub * MOBA_BLOCK)) * MOBA_PAGES_PER_STEP)
    per_f, per_b = _plan_page_stream(pages.shape[0], steps_f, steps_b, moba_pages)
    pf, pb, means = _project_both(xp, nw, w_in_b, epi_w, tm=tm_p, tn=tn_p,
                                  stream=(pages, cache_k[layer], per_f, per_b))
    y_a = _gmlp(pf, pb, w_spatial[layer], b_spatial[layer].T, tm=512)
    y_b, km_loop, km_steps = _moba(pf, pb, slopes, sub=sub,
                                   stream=(pages, cache_k[layer], per_f * steps_f + per_b * steps_b,
                                           MOBA_PAGES_PER_ITER, MOBA_PAGES_PER_STEP))
    k_mean = jnp.concatenate(means + [km_loop, km_steps], axis=0)
    y_prompt = _merge(xp, y_a, y_b, pb, w_oa_b, w_ob_b, w_o_b, tm=256)

    def section(arr, c):
        return arr[:, c * SECTION:(c + 1) * SECTION]

    k, vb = section(pf, F_K), section(pf, F_VB)

    m_s = n_bat * n_new
    xs = x_sample.reshape(m_s, D_MODEL)
    pf_s, pb_s = _project_both(xs, nw, w_in_b, epi_w, tm=m_s, tn=512)
    v_s, q_s, k_s, vb_s = (section(pf_s, c) for c in (F_V, F_Q, F_K, F_VB))
    u_s, za_s, zb_s = (section(pb_s, c) for c in (P_U, P_ZA, P_ZB))
    w4 = w_spatial[layer][:, :n_new, :n_new]
    wrow = jnp.repeat(w4.transpose(1, 2, 0), GROUP_DIM, axis=-1).reshape(n_new * n_new, D_A)
    brow = jnp.repeat(b_spatial[layer][:, :n_new].T, GROUP_DIM, axis=-1)
    three = (n_bat, n_new, D_A)
    ya_s = _gmlp_dec(u_s.reshape(three), v_s.reshape(three), za_s.reshape(three), wrow, brow)
    q3 = q_s.reshape(n_bat, n_new, D_B)
    n_full = n_pages // PAGES_PER_BLOCK
    idx_pad = _gate_topk(q3, k_mean.reshape(n_bat, n_full, N_HEADS, HEAD_DIM))
    idx = idx_pad[:, :, :MOBA_TOPK].reshape(n_bat, N_HEADS * n_new * MOBA_TOPK)
    yb_s = _moba_dec(page_table, idx, slopes, q3, k_s.reshape(n_bat, n_new, D_B),
                     vb_s.reshape(n_bat, n_new, D_B), zb_s.reshape(n_bat, n_new, D_B),
                     cache_k[layer], cache_v[layer], past=past)
    y_sample = _merge(xs, ya_s.reshape(m_s, D_A), yb_s.reshape(m_s, D_B), pb_s,
                      w_oa_b, w_ob_b, w_o_b, tm=m_s)

    return (y_prompt.reshape(x_prompt.shape),
            y_sample.reshape(x_sample.shape),
            k.reshape(depth, n_bp, seq, N_HEADS, HEAD_DIM),
            vb.reshape(depth, n_bp, seq, N_HEADS, HEAD_DIM),
            k_s.reshape(depth, n_bat, n_new, N_HEADS, HEAD_DIM),
            vb_s.reshape(depth, n_bat, n_new, N_HEADS, HEAD_DIM),
            v_s.reshape(depth, n_bat, n_new, D_A))
```

```python
import functools

import jax
import jax.numpy as jnp
import numpy as np
from jax import lax
from jax.experimental import pallas as pl
from jax.experimental.pallas import tpu as pltpu

D_MODEL = 2048
A_GROUPS = 8
GROUP_DIM = 128
D_A = A_GROUPS * GROUP_DIM
CHUNK = 128
N_HEADS = 8
HEAD_DIM = 128
D_B = N_HEADS * HEAD_DIM
MOBA_BLOCK = 256
MOBA_TOPK = 3
PAGE_SIZE = 128
PAGES_PER_BLOCK = MOBA_BLOCK // PAGE_SIZE
DEC_SLOTS = 3
DEC_HEADS_PER_STEP = 2
MOBA_PAGES_PER_ITER = 4
MOBA_PAGES_PER_STEP = 12
MOBA_LOOP_SLOTS = 3
SECTION = 1024
N_SECTIONS = 11
D_IN = N_SECTIONS * SECTION
EPS = 1e-6
NEG = -1e30
REMOVED = -3e38
SCALE = HEAD_DIM ** -0.5
VMEM_LIMIT = 56 * 1024 * 1024

_NT = (((1,), (1,)), ((), ()))
_BF16 = jnp.bfloat16
_F32 = jnp.float32


def _silu(z):
    return z * jax.nn.sigmoid(z)


W_SECTIONS_F32 = (1, 3, 4, 5)
NORMED_F32 = (1, 1, 1, 0)
W_SECTIONS_BF16 = (7, 8, 9, 10, 0, 2, 6)
F_V, F_Q, F_K, F_VB = 0, 1, 2, 3
P_GA, P_GB, P_U, P_ZA, P_ZB = 0, 2, 4, 5, 6


def _proj_kernel(wcol_ref, normed_ref, pages_ref, x_ref, nw_ref, w_ref, ew_ref, *rest,
                 tn, grouped, first_page, pages_per_step):
    if pages_per_step:
        ck_ref, o_ref, km_ref, h_ref, pbuf, sems = rest
    else:
        o_ref, h_ref = rest
    j = pl.program_id(1)

    if pages_per_step:
        n_j = pl.num_programs(1)
        step = pl.program_id(0) * n_j + j
        slot = step % 2

        def page_copies(st, sl):
            base = first_page + st * pages_per_step
            return [pltpu.make_async_copy(ck_ref.at[pages_ref[base + r]], pbuf.at[sl, r], sems.at[sl])
                    for r in range(pages_per_step)]

        @pl.when(step == 0)
        def _():
            for cp in page_copies(step, slot):
                cp.start()

        for cp in page_copies(step, slot):
            cp.wait()

        @pl.when(step + 1 < pl.num_programs(0) * n_j)
        def _():
            for cp in page_copies(step + 1, 1 - slot):
                cp.start()

    @pl.when(j == 0)
    def _():
        xf = x_ref[...]
        ms = jnp.mean(xf * xf, axis=-1, keepdims=True)
        h_ref[...] = (xf * lax.rsqrt(ms + EPS) * nw_ref[...]).astype(_BF16)

    acc = jnp.dot(h_ref[...], w_ref[...], preferred_element_type=_F32)
    if grouped:
        use_norm = normed_ref[j] != 0
        for c in range(tn // GROUP_DIM):
            cs = slice(c * GROUP_DIM, (c + 1) * GROUP_DIM)
            blk = acc[:, cs]
            ms = jnp.mean(blk * blk, axis=-1, keepdims=True)
            scale = jnp.where(use_norm, lax.rsqrt(ms + EPS), 1.0)
            o_ref[:, cs] = (blk * scale * ew_ref[:, cs]).astype(o_ref.dtype)
    else:
        o_ref[...] = acc.astype(o_ref.dtype)

    for blk_i in range(pages_per_step // PAGES_PER_BLOCK):
        tot = jnp.sum(pbuf[slot, PAGES_PER_BLOCK * blk_i], axis=0)
        for r in range(1, PAGES_PER_BLOCK):
            tot = tot + jnp.sum(pbuf[slot, PAGES_PER_BLOCK * blk_i + r], axis=0)
        km_ref[blk_i] = tot * (1.0 / MOBA_BLOCK)


def _project(x2d, norm_w, w_bf16, epi_w, *, sections, normed, out_dtype, tm, tn, stream=None):
    m = x2d.shape[0]
    tiles = SECTION // tn
    n_tiles = len(sections) * tiles
    n_steps = (m // tm) * n_tiles
    wcol = np.array([s * tiles + t for s in sections for t in range(tiles)], np.int32)
    flags = np.array([f for f in (normed or (0,) * len(sections)) for _ in range(tiles)], np.int32)
    in_specs = [
        pl.BlockSpec((tm, D_MODEL), lambda i, j, *_: (i, 0)),
        pl.BlockSpec((1, D_MODEL), lambda i, j, *_: (0, 0)),
        pl.BlockSpec((D_MODEL, tn), lambda i, j, wc, *_: (0, wc[j])),
        pl.BlockSpec((1, tn), lambda i, j, wc, *_: (0, wc[j])),
    ]
    out_specs = [pl.BlockSpec((tm, tn), lambda i, j, *_: (i, j))]
    out_shape = [jax.ShapeDtypeStruct((m, len(sections) * SECTION), out_dtype)]
    scratch = [pltpu.VMEM((tm, D_MODEL), _BF16)]
    operands = [x2d, norm_w, w_bf16, epi_w]
    if stream is None:
        pages, first_page, pages_per_step = jnp.zeros((1,), jnp.int32), 0, 0
    else:
        pages, cache_k, first_page, pages_per_step = stream
        assert pages_per_step % PAGES_PER_BLOCK == 0 and first_page % PAGES_PER_BLOCK == 0
        blocks_per_step = pages_per_step // PAGES_PER_BLOCK
        in_specs.append(pl.BlockSpec(memory_space=pl.ANY))
        operands.append(cache_k)
        out_specs.append(pl.BlockSpec((blocks_per_step, N_HEADS, HEAD_DIM), lambda i, j, *_: (i * n_tiles + j, 0, 0)))
        out_shape.append(jax.ShapeDtypeStruct((n_steps * blocks_per_step, N_HEADS, HEAD_DIM), _F32))
        scratch += [pltpu.VMEM((2, pages_per_step, PAGE_SIZE, N_HEADS, HEAD_DIM), _F32),
                    pltpu.SemaphoreType.DMA((2,))]
    grid_spec = pltpu.PrefetchScalarGridSpec(
        num_scalar_prefetch=3, grid=(m // tm, n_tiles),
        in_specs=in_specs, out_specs=out_specs, scratch_shapes=scratch)
    outs = pl.pallas_call(
        functools.partial(_proj_kernel, tn=tn, grouped=normed is not None,
                          first_page=first_page, pages_per_step=pages_per_step),
        grid_spec=grid_spec,
        out_shape=out_shape,
        compiler_params=pltpu.CompilerParams(
            dimension_semantics=("arbitrary", "arbitrary"), vmem_limit_bytes=VMEM_LIMIT),
        name="proj_f32" if normed is not None else "proj_bf16",
    )(jnp.asarray(wcol), jnp.asarray(flags), pages, *operands)
    return outs if stream is not None else outs[0]


def _proj_steps(m, tm, tn):
    per_section = (m // tm) * (SECTION // tn)
    return per_section * len(W_SECTIONS_F32), per_section * len(W_SECTIONS_BF16)


def _project_both(x2d, norm_w, w_bf16, epi_w, *, tm, tn, stream=None):
    common = dict(tm=tm, tn=tn)
    f32_args = dict(sections=W_SECTIONS_F32, normed=NORMED_F32, out_dtype=_F32, **common)
    bf16_args = dict(sections=W_SECTIONS_BF16, normed=None, out_dtype=_BF16, **common)
    if stream is None:
        return _project(x2d, norm_w, w_bf16, epi_w, **f32_args), _project(x2d, norm_w, w_bf16, epi_w, **bf16_args)
    pages, cache_k, per_f, per_b = stream
    steps_f, _ = _proj_steps(x2d.shape[0], tm, tn)
    means = []
    pf = _project(x2d, norm_w, w_bf16, epi_w, stream=(pages, cache_k, 0, per_f) if per_f else None, **f32_args)
    if per_f:
        pf, km = pf
        means.append(km)
    pb = _project(x2d, norm_w, w_bf16, epi_w, stream=(pages, cache_k, per_f * steps_f, per_b) if per_b else None,
                  **bf16_args)
    if per_b:
        pb, km = pb
        means.append(km)
    return pf, pb, means


def _plan_page_stream(n_pages, steps_f, steps_b, moba_pages):
    rest = n_pages - moba_pages
    assert rest >= 0
    best = None
    for per_f in range(0, rest // steps_f + 1, PAGES_PER_BLOCK):
        per_b, rem = divmod(rest - per_f * steps_f, steps_b)
        if rem == 0 and per_b % PAGES_PER_BLOCK == 0 and (best is None or max(per_f, per_b) < max(best)):
            best = (per_f, per_b)
    assert best is not None, "page count does not split over the projection / attention steps"
    return best


def _gmlp_kernel(u_ref, v_ref, za_ref, ws_ref, bs_ref, ya_ref, *, n_chunks):
    row = lax.broadcasted_iota(jnp.int32, (CHUNK, CHUNK), 0)
    col = lax.broadcasted_iota(jnp.int32, (CHUNK, CHUNK), 1)
    tril = row >= col
    for g in range(A_GROUPS):
        w_g = jnp.where(tril, ws_ref[g], 0.0).astype(_BF16)
        b_g = bs_ref[:, g:g + 1]
        cs = slice(g * GROUP_DIM, (g + 1) * GROUP_DIM)
        for c in range(n_chunks):
            rs = slice(c * CHUNK, (c + 1) * CHUNK)
            mixed = jnp.dot(w_g, v_ref[rs, cs].astype(_BF16), preferred_element_type=_F32) + b_g
            y = u_ref[rs, cs].astype(_F32) * mixed * _silu(za_ref[rs, cs].astype(_F32))
            ya_ref[rs, cs] = y.astype(ya_ref.dtype)


def _gmlp(pf, pb, w_spatial, b_spatial_t, *, tm):
    m = pf.shape[0]

    def section(c):
        return pl.BlockSpec((tm, D_A), lambda i: (i, c))

    row_spec = section(0)
    return pl.pallas_call(
        functools.partial(_gmlp_kernel, n_chunks=tm // CHUNK),
        grid=(m // tm,),
        in_specs=[section(P_U), section(F_V), section(P_ZA),
                  pl.BlockSpec((A_GROUPS, CHUNK, CHUNK), lambda i: (0, 0, 0)),
                  pl.BlockSpec((CHUNK, A_GROUPS), lambda i: (0, 0))],
        out_specs=row_spec,
        out_shape=jax.ShapeDtypeStruct((m, D_A), _BF16),
        compiler_params=pltpu.CompilerParams(
            dimension_semantics=("arbitrary",), vmem_limit_bytes=VMEM_LIMIT),
        name="gmlp",
    )(pb, pf, pb, w_spatial, b_spatial_t)


def _gmlp_dec_kernel(u_ref, v_ref, za_ref, wrow_ref, brow_ref, ya_ref, *, n_new):
    for t in range(n_new):
        mixed = brow_ref[t:t + 1, :]
        for s in range(t + 1):
            mixed = mixed + wrow_ref[t * n_new + s:t * n_new + s + 1, :] * v_ref[:, s, :]
        y = u_ref[:, t, :].astype(_F32) * mixed * _silu(za_ref[:, t, :].astype(_F32))
        ya_ref[:, t, :] = y.astype(ya_ref.dtype)


def _gmlp_dec(u, v, z_a, wrow, brow):
    n_bat, n_new, _ = v.shape
    full3 = pl.BlockSpec((n_bat, n_new, D_A), lambda i: (0, 0, 0))
    return pl.pallas_call(
        functools.partial(_gmlp_dec_kernel, n_new=n_new),
        grid=(1,),
        in_specs=[full3, full3, full3,
                  pl.BlockSpec(wrow.shape, lambda i: (0, 0)),
                  pl.BlockSpec(brow.shape, lambda i: (0, 0))],
        out_specs=full3,
        out_shape=jax.ShapeDtypeStruct((n_bat, n_new, D_A), _BF16),
        name="gmlp_dec",
    )(u, v, z_a, wrow, brow)


def _topk_mask(gate, col, n_valid, n_cols):
    g = jnp.where(col < n_valid, gate, NEG)
    sel = jnp.zeros(gate.shape, jnp.bool_)
    for j in range(MOBA_TOPK):
        m = jnp.max(g, axis=-1, keepdims=True)
        idx = jnp.min(jnp.where(g == m, col, n_cols), axis=-1, keepdims=True)
        pick = col == idx
        sel = sel | (pick & (j < n_valid))
        g = jnp.where(pick, REMOVED, g)
    return sel


def _bf16_pieces(x, n):
    out = []
    for _ in range(n):
        piece = float(np.asarray(x, np.float32).astype(_BF16).astype(np.float32))
        out.append(piece)
        x = x - piece
    return tuple(out)


_LOG2E = 1.4426950408889634
_LOG2E_PIECES = _bf16_pieces(_LOG2E, 3)
_N_PIECES = len(_LOG2E_PIECES)


def _lane_values(lane, first, values):
    out = jnp.zeros(lane.shape, _F32)
    for p, val in enumerate(values):
        out = jnp.where(lane == first + p, val, out)
    return out


def _moba_kernel(pages_ref, slopes_ref, q_ref, k_ref, v_ref, zb_ref, *rest,
                 n_blocks, sub, n_q, first_page, pages_per_iter, pages_per_step):
    if pages_per_iter:
        (ck_ref, o_ref, kmo_ref, kms_ref, kaug_ref, vb_ref, km_ref, qa_ref, m_ref, l_ref, acc_ref, sa_ref, sb_ref,
         pbuf, sems, pbuf2, sems2) = rest
    else:
        o_ref, kaug_ref, vb_ref, km_ref, qa_ref, m_ref, l_ref, acc_ref, sa_ref, sb_ref = rest
    h = pl.program_id(0)
    qi = pl.program_id(1)
    blk = MOBA_BLOCK

    if pages_per_iter:
        grid_step = h * n_q + qi
        step_slot = grid_step % 2
        first_step_page = first_page + N_HEADS * (sub // 2) * (n_q * (n_q - 1) // 2) * pages_per_iter

        def step_copies(st, sl):
            base = first_step_page + st * pages_per_step
            return [pltpu.make_async_copy(ck_ref.at[pages_ref[base + r]], pbuf2.at[sl, r], sems2.at[sl])
                    for r in range(pages_per_step)]

        @pl.when(grid_step == 0)
        def _():
            for cp in step_copies(grid_step, step_slot):
                cp.start()

        for cp in step_copies(grid_step, step_slot):
            cp.wait()

        @pl.when(grid_step + 1 < pl.num_programs(0) * n_q)
        def _():
            for cp in step_copies(grid_step + 1, 1 - step_slot):
                cp.start()
    lane_hi = n_blocks
    lane_lo = lane_hi + _N_PIECES
    lane_t0 = lane_lo + _N_PIECES
    slope = slopes_ref[h]
    coef = [slope * c for c in _LOG2E_PIECES]
    lane = lax.broadcasted_iota(jnp.int32, (blk, HEAD_DIM), 1)
    hi_lanes = (lane >= lane_hi) & (lane < lane_lo)

    @pl.when(qi == 0)
    def _():
        km_ref[...] = jnp.zeros(km_ref.shape, _F32)
        row_f = lax.broadcasted_iota(jnp.int32, (blk, HEAD_DIM), 0).astype(_F32)
        fixed = (_lane_values(lane, lane_t0, [-c * blk for c in coef])
                 + jnp.where((lane >= lane_lo) & (lane < lane_t0), row_f, 0.0))

        def body(n, carry):
            rows = pl.ds(pl.multiple_of(n * blk, blk), blk)
            kf = k_ref[rows, :]
            aug = fixed + jnp.where(lane == n, NEG, 0.0) + jnp.where(hi_lanes, jnp.asarray(n * blk, _F32), 0.0)
            kaug_ref[rows, :HEAD_DIM] = kf.astype(_BF16)
            kaug_ref[rows, HEAD_DIM:] = aug.astype(_BF16)
            vb_ref[rows, :] = v_ref[rows, :].astype(_BF16)
            km_ref[pl.ds(n, 1), :] = jnp.mean(kf, axis=0, keepdims=True)
            return carry
        lax.fori_loop(0, n_blocks, body, 0)

    q_consts = (_lane_values(lane, lane_hi, coef) + _lane_values(lane, lane_lo, coef)
                + jnp.where((lane >= lane_t0) & (lane < lane_t0 + _N_PIECES), jnp.asarray(qi * sub, _F32), 0.0))
    for r in range(sub):
        rs = slice(r * blk, (r + 1) * blk)
        qf = q_ref[rs, :]
        gate = lax.dot_general(qf, km_ref[...], _NT, precision=lax.Precision.HIGHEST,
                               preferred_element_type=_F32)
        own = qi * sub + r
        open_blocks = _topk_mask(gate, lane, own, HEAD_DIM) | (lane == own)
        qa_ref[rs, :HEAD_DIM] = (qf * (SCALE * _LOG2E)).astype(_BF16)
        qa_ref[rs, HEAD_DIM:] = jnp.where(lane < n_blocks, jnp.where(open_blocks, 0.0, 1.0), q_consts).astype(_BF16)

    def scores(r, k_tile):
        rs = slice(r * blk, (r + 1) * blk)
        return lax.dot_general(qa_ref[rs, :], k_tile, _NT, preferred_element_type=_F32)

    def attend(r, s, v_tile, first):
        rs = slice(r * blk, (r + 1) * blk)
        s0, s1 = s[:, :HEAD_DIM], s[:, HEAD_DIM:]
        mx = jnp.max(jnp.maximum(s0, s1), axis=-1, keepdims=True)
        if first:
            m_new = jnp.broadcast_to(mx, (blk, HEAD_DIM))
        else:
            m_old = m_ref[rs, :]
            m_new = jnp.maximum(m_old, mx)
        p0 = jnp.exp2(s0 - m_new)
        p1 = jnp.exp2(s1 - m_new)
        row_sum = jnp.sum(p0 + p1, axis=-1, keepdims=True)
        pv = jnp.dot(jnp.concatenate([p0, p1], axis=-1).astype(_BF16), v_tile, preferred_element_type=_F32)
        if first:
            l_ref[rs, :] = jnp.broadcast_to(row_sum, (blk, HEAD_DIM))
            acc_ref[rs, :] = pv
        else:
            alpha = jnp.exp2(m_old - m_new)
            l_ref[rs, :] = alpha * l_ref[rs, :] + row_sum
            acc_ref[rs, :] = alpha * acc_ref[rs, :] + pv
        m_ref[rs, :] = m_new

    ri = lax.broadcasted_iota(jnp.int32, (blk, blk), 0)
    ci = lax.broadcasted_iota(jnp.int32, (blk, blk), 1)
    causal = ri >= ci
    for r in range(sub):
        for r2 in [r] + list(range(r)):
            rows = pl.ds(pl.multiple_of((qi * sub + r2) * blk, blk), blk)
            s = scores(r, kaug_ref[rows, :])
            attend(r, jnp.where(causal, s, NEG) if r2 == r else s, vb_ref[rows, :], first=(r2 == r))

    if pages_per_iter:
        for b in range(pages_per_step // PAGES_PER_BLOCK):
            tot = jnp.sum(pbuf2[step_slot, PAGES_PER_BLOCK * b], axis=0)
            for r in range(1, PAGES_PER_BLOCK):
                tot = tot + jnp.sum(pbuf2[step_slot, PAGES_PER_BLOCK * b + r], axis=0)
            kms_ref[b] = tot * (1.0 / MOBA_BLOCK)

    n_past = qi * sub

    def block_rows(n):
        return pl.ds(pl.multiple_of(n * blk, blk), blk)

    def produce(n, dst_ref):
        k_tile = kaug_ref[block_rows(n), :]
        for r in range(sub):
            dst_ref[r * blk:(r + 1) * blk, :] = scores(r, k_tile)

    def consume(n, src_ref):
        v_tile = vb_ref[block_rows(n), :]
        for r in range(sub):
            attend(r, src_ref[r * blk:(r + 1) * blk, :], v_tile, first=False)

    @pl.when(n_past > 0)
    def _():
        produce(0, sa_ref)

    iters_per_head = (sub // 2) * (n_q * (n_q - 1) // 2)
    n_iters = N_HEADS * iters_per_head
    it_base = h * iters_per_head + (sub // 2) * ((qi * (qi - 1)) // 2)
    n_slots = pbuf.shape[0] if pages_per_iter else 1
    ahead = n_slots - 1

    def chunk_copies(it, sl):
        base = first_page + it * pages_per_iter
        return [pltpu.make_async_copy(ck_ref.at[pages_ref[base + r]], pbuf.at[sl, r], sems.at[sl])
                for r in range(pages_per_iter)]

    if pages_per_iter:
        @pl.when((h == 0) & (qi == 0))
        def _():
            for c in range(ahead):
                for cp in chunk_copies(c, c):
                    cp.start()

    def body(j, carry):
        if pages_per_iter:
            it = it_base + j
            slot = it % n_slots
            for cp in chunk_copies(jnp.minimum(it + ahead, n_iters - 1), (it + ahead) % n_slots):
                cp.start()
            for cp in chunk_copies(it, slot):
                cp.wait()
        n0 = 2 * j
        produce(n0 + 1, sb_ref)
        consume(n0, sa_ref)
        produce(jnp.minimum(n0 + 2, n_past - 1), sa_ref)
        consume(n0 + 1, sb_ref)
        blocks_per_iter = pages_per_iter // PAGES_PER_BLOCK
        for b in range(blocks_per_iter):
            tot = jnp.sum(pbuf[slot, PAGES_PER_BLOCK * b], axis=0)
            for r in range(1, PAGES_PER_BLOCK):
                tot = tot + jnp.sum(pbuf[slot, PAGES_PER_BLOCK * b + r], axis=0)
            kmo_ref[it * blocks_per_iter + b] = tot * (1.0 / MOBA_BLOCK)
        return carry
    lax.fori_loop(0, n_past // 2, body, 0)

    if pages_per_iter:
        @pl.when((h == pl.num_programs(0) - 1) & (qi == n_q - 1))
        def _():
            for c in range(ahead):
                for cp in chunk_copies(n_iters - 1, (n_iters + c) % n_slots):
                    cp.wait()

    o = acc_ref[...] / l_ref[...]
    o_ref[...] = (o * _silu(zb_ref[...].astype(_F32))).astype(o_ref.dtype)


def _moba_iterations(n_pos, sub):
    n_q = n_pos // (sub * MOBA_BLOCK)
    return N_HEADS * (sub // 2) * (n_q * (n_q - 1) // 2)


def _moba(pf, pb, slopes, *, sub, stream=None):
    n_pos = pf.shape[0]
    n_blocks = n_pos // MOBA_BLOCK
    tq = sub * MOBA_BLOCK
    n_q = n_pos // tq
    assert n_pos % tq == 0 and sub % 2 == 0 and n_blocks + 3 * _N_PIECES <= HEAD_DIM

    def tile_of(section):
        return pl.BlockSpec((tq, HEAD_DIM), lambda h, i, *_: (i, section * N_HEADS + h))

    def head_all(section):
        return pl.BlockSpec((n_pos, HEAD_DIM), lambda h, i, *_: (0, section * N_HEADS + h))

    in_specs = [pl.BlockSpec(memory_space=pltpu.SMEM), tile_of(F_Q), head_all(F_K), head_all(F_VB), tile_of(P_ZB)]
    operands = [slopes, pf, pf, pf, pb]
    out_specs = [tile_of(0)]
    out_shape = [jax.ShapeDtypeStruct((n_pos, D_B), _BF16)]
    scratch = [pltpu.VMEM((n_pos, 2 * HEAD_DIM), _BF16),
               pltpu.VMEM((n_pos, HEAD_DIM), _BF16),
               pltpu.VMEM((HEAD_DIM, HEAD_DIM), _F32),
               pltpu.VMEM((tq, 2 * HEAD_DIM), _BF16),
               pltpu.VMEM((tq, HEAD_DIM), _F32),
               pltpu.VMEM((tq, HEAD_DIM), _F32),
               pltpu.VMEM((tq, HEAD_DIM), _F32),
               pltpu.VMEM((tq, MOBA_BLOCK), _F32),
               pltpu.VMEM((tq, MOBA_BLOCK), _F32)]
    if stream is None:
        pages, first_page, pages_per_iter, pages_per_step = jnp.zeros((1,), jnp.int32), 0, 0, 0
    else:
        pages, cache_k, first_page, pages_per_iter, pages_per_step = stream
        n_iters = _moba_iterations(n_pos, sub)
        assert n_iters >= MOBA_LOOP_SLOTS and first_page % PAGES_PER_BLOCK == 0
        assert pages_per_iter > 0 and pages_per_iter % PAGES_PER_BLOCK == 0
        assert pages_per_step > 0 and pages_per_step % PAGES_PER_BLOCK == 0
        n_km = n_iters * pages_per_iter // PAGES_PER_BLOCK
        km_per_step = pages_per_step // PAGES_PER_BLOCK
        in_specs.append(pl.BlockSpec(memory_space=pl.ANY))
        operands.append(cache_k)
        out_specs += [pl.BlockSpec((n_km, N_HEADS, HEAD_DIM), lambda h, i, *_: (0, 0, 0)),
                      pl.BlockSpec((km_per_step, N_HEADS, HEAD_DIM), lambda h, i, *_: (h * n_q + i, 0, 0))]
        out_shape += [jax.ShapeDtypeStruct((n_km, N_HEADS, HEAD_DIM), _F32),
                      jax.ShapeDtypeStruct((N_HEADS * n_q * km_per_step, N_HEADS, HEAD_DIM), _F32)]
        scratch += [pltpu.VMEM((MOBA_LOOP_SLOTS, pages_per_iter, PAGE_SIZE, N_HEADS, HEAD_DIM), _F32),
                    pltpu.SemaphoreType.DMA((MOBA_LOOP_SLOTS,)),
                    pltpu.VMEM((2, pages_per_step, PAGE_SIZE, N_HEADS, HEAD_DIM), _F32),
                    pltpu.SemaphoreType.DMA((2,))]
    grid_spec = pltpu.PrefetchScalarGridSpec(
        num_scalar_prefetch=1, grid=(N_HEADS, n_q),
        in_specs=in_specs, out_specs=out_specs, scratch_shapes=scratch)
    outs = pl.pallas_call(
        functools.partial(_moba_kernel, n_blocks=n_blocks, sub=sub, n_q=n_q, first_page=first_page,
                          pages_per_iter=pages_per_iter, pages_per_step=pages_per_step),
        grid_spec=grid_spec,
        out_shape=out_shape,
        compiler_params=pltpu.CompilerParams(
            dimension_semantics=("arbitrary", "arbitrary"), vmem_limit_bytes=VMEM_LIMIT),
        name="moba",
    )(pages, *operands)
    return outs if stream is not None else outs[0]


def _gate_kernel(q_ref, km_ref, idx_ref, *, n_full, n_new):
    col = lax.broadcasted_iota(jnp.int32, (n_new, n_full), 1)
    lane = lax.broadcasted_iota(jnp.int32, (n_new, 128), 1)
    for h in range(N_HEADS):
        qh = q_ref[0, :, h * HEAD_DIM:(h + 1) * HEAD_DIM]
        g = lax.dot_general(qh, km_ref[0, :, h, :], _NT, precision=lax.Precision.HIGHEST,
                            preferred_element_type=_F32)
        tile = jnp.zeros((n_new, 128), jnp.int32)
        for j in range(MOBA_TOPK):
            m = jnp.max(g, axis=-1, keepdims=True)
            idx = jnp.min(jnp.where(g == m, col, n_full), axis=-1, keepdims=True)
            tile = jnp.where(lane == j, idx, tile)
            g = jnp.where(col == idx, REMOVED, g)
        idx_ref[0, h * n_new:(h + 1) * n_new, :] = tile


def _gate_topk(q3, k_mean):
    n_bat, n_new, _ = q3.shape
    n_full = k_mean.shape[1]
    return pl.pallas_call(
        functools.partial(_gate_kernel, n_full=n_full, n_new=n_new),
        grid=(n_bat,),
        in_specs=[pl.BlockSpec((1, n_new, D_B), lambda b: (b, 0, 0)),
                  pl.BlockSpec((1, n_full, N_HEADS, HEAD_DIM), lambda b: (b, 0, 0, 0))],
        out_specs=pl.BlockSpec((1, N_HEADS * n_new, 128), lambda b: (b, 0, 0)),
        out_shape=jax.ShapeDtypeStruct((n_bat, N_HEADS * n_new, 128), jnp.int32),
        name="gate",
    )(q3, k_mean)


def _moba_dec_kernel(pt_ref, idx_ref, slopes_ref, q_ref, kn_ref, vn_ref, zb_ref, ck_ref, cv_ref,
                     o_ref, kbuf, vbuf, sems, *, n_new, past, heads_per_step):
    per_t = MOBA_TOPK * PAGES_PER_BLOCK
    n_flat = MOBA_TOPK * MOBA_BLOCK
    n_pg = n_new * per_t
    b = pl.program_id(0)
    hg = pl.program_id(1)
    n_groups = pl.num_programs(1)
    step = b * n_groups + hg
    n_steps = pl.num_programs(0) * n_groups
    n_slots = kbuf.shape[0]
    ahead = n_slots - 1
    slot = step % n_slots

    def page_copies(st, sl):
        bb, gg = st // n_groups, st % n_groups
        copies = []
        for j in range(heads_per_step):
            hh = gg * heads_per_step + j
            for t in range(n_new):
                for c in range(per_t):
                    blk_idx = idx_ref[bb, (hh * n_new + t) * MOBA_TOPK + c // PAGES_PER_BLOCK]
                    page = pt_ref[bb, blk_idx * PAGES_PER_BLOCK + c % PAGES_PER_BLOCK]
                    dst = j * n_pg + t * per_t + c
                    copies.append(pltpu.make_async_copy(ck_ref.at[page, :, hh, :], kbuf.at[sl, dst], sems.at[sl]))
                    copies.append(pltpu.make_async_copy(cv_ref.at[page, :, hh, :], vbuf.at[sl, dst], sems.at[sl]))
        return copies

    @pl.when(step == 0)
    def _():
        for st in range(ahead):
            @pl.when(st < n_steps)
            def _(st=st):
                for cp in page_copies(st, st % n_slots):
                    cp.start()

    @pl.when(step + ahead < n_steps)
    def _():
        for cp in page_copies(step + ahead, (step + ahead) % n_slots):
            cp.start()

    for cp in page_copies(step, slot):
        cp.wait()

    row = lax.broadcasted_iota(jnp.int32, (n_new, n_flat), 0)
    lane = lax.broadcasted_iota(jnp.int32, (n_new, n_flat), 1)
    lane_o = lax.broadcasted_iota(jnp.int32, (n_new, 128), 1)
    row_o = lax.broadcasted_iota(jnp.int32, (n_new, 128), 0)
    row_d = lax.broadcasted_iota(jnp.int32, (n_new, HEAD_DIM), 0)

    for j in range(heads_per_step):
        h = hg * heads_per_step + j
        hs = slice(j * HEAD_DIM, (j + 1) * HEAD_DIM)
        slope = slopes_ref[h]
        qf = q_ref[0, :, hs]
        qb = qf.astype(_BF16)

        s_sel = jnp.zeros((n_new, n_flat), _F32)
        base = jnp.zeros((n_new, n_flat), jnp.int32)
        for t in range(n_new):
            pages_t = slice(j * n_pg + t * per_t, j * n_pg + (t + 1) * per_t)
            k_cat = kbuf[slot, pages_t].reshape(n_flat, HEAD_DIM).astype(_BF16)
            s_t = lax.dot_general(qb, k_cat, _NT, preferred_element_type=_F32)
            s_sel = jnp.where(row == t, s_t, s_sel)
            for c in range(per_t):
                blk_idx = idx_ref[b, (h * n_new + t) * MOBA_TOPK + c // PAGES_PER_BLOCK]
                start = blk_idx * MOBA_BLOCK + (c % PAGES_PER_BLOCK) * PAGE_SIZE - c * PAGE_SIZE
                in_page = (row == t) & (lane >= c * PAGE_SIZE) & (lane < (c + 1) * PAGE_SIZE)
                base = jnp.where(in_page, start, base)
        dist_sel = (past + row - (base + lane)).astype(_F32)
        s_sel = s_sel * SCALE - slope * dist_sel

        knf = kn_ref[0, :, hs]
        s_own = jnp.full((n_new, 128), NEG, _F32)
        for t2 in range(n_new):
            dots = jnp.sum(qf * knf[t2:t2 + 1, :], axis=-1, keepdims=True)
            s_own = jnp.where(lane_o == t2, dots, s_own)
        dist_own = (row_o - lane_o).astype(_F32)
        s_own = jnp.where((lane_o < n_new) & (row_o >= lane_o), s_own * SCALE - slope * dist_own, NEG)

        m = jnp.maximum(jnp.max(s_sel, axis=-1, keepdims=True), jnp.max(s_own, axis=-1, keepdims=True))
        p_sel = jnp.exp(s_sel - m)
        p_own = jnp.exp(s_own - m)
        l = jnp.sum(p_sel, axis=-1, keepdims=True) + jnp.sum(p_own, axis=-1, keepdims=True)
        p_sel = (p_sel / l).astype(_BF16)
        p_own = p_own / l

        o = jnp.zeros((n_new, HEAD_DIM), _F32)
        for t in range(n_new):
            pages_t = slice(j * n_pg + t * per_t, j * n_pg + (t + 1) * per_t)
            v_cat = vbuf[slot, pages_t].reshape(n_flat, HEAD_DIM).astype(_BF16)
            o_t = jnp.dot(p_sel, v_cat, preferred_element_type=_F32)
            o = jnp.where(row_d == t, o_t, o)
        vnf = vn_ref[0, :, hs]
        for t2 in range(n_new):
            w = jnp.sum(jnp.where(lane_o == t2, p_own, 0.0), axis=-1, keepdims=True)
            o = o + w * vnf[t2:t2 + 1, :]
        o_ref[0, :, hs] = (o * _silu(zb_ref[0, :, hs].astype(_F32))).astype(o_ref.dtype)


def _moba_dec(page_table, idx, slopes, q3, k3, v3, zb3, cache_k, cache_v, *, past):
    n_bat, n_new, _ = q3.shape
    n_pg = DEC_HEADS_PER_STEP * n_new * MOBA_TOPK * PAGES_PER_BLOCK
    tok = pl.BlockSpec((1, n_new, DEC_HEADS_PER_STEP * HEAD_DIM), lambda b, g, pt, ix: (b, 0, g))
    hbm = pl.BlockSpec(memory_space=pl.ANY)
    grid_spec = pltpu.PrefetchScalarGridSpec(
        num_scalar_prefetch=2,
        grid=(n_bat, N_HEADS // DEC_HEADS_PER_STEP),
        in_specs=[pl.BlockSpec(memory_space=pltpu.SMEM), tok, tok, tok, tok, hbm, hbm],
        out_specs=tok,
        scratch_shapes=[pltpu.VMEM((DEC_SLOTS, n_pg, PAGE_SIZE, HEAD_DIM), _F32),
                        pltpu.VMEM((DEC_SLOTS, n_pg, PAGE_SIZE, HEAD_DIM), _F32),
                        pltpu.SemaphoreType.DMA((DEC_SLOTS,))],
    )
    return pl.pallas_call(
        functools.partial(_moba_dec_kernel, n_new=n_new, past=past, heads_per_step=DEC_HEADS_PER_STEP),
        grid_spec=grid_spec,
        out_shape=jax.ShapeDtypeStruct((n_bat, n_new, D_B), _BF16),
        compiler_params=pltpu.CompilerParams(
            dimension_semantics=("arbitrary", "arbitrary"), vmem_limit_bytes=VMEM_LIMIT),
        name="moba_dec",
    )(page_table, idx, slopes, q3, k3, v3, zb3, cache_k, cache_v)


def _merge_kernel(x_ref, ya_ref, yb_ref, ga_ref, gb_ref, woa_ref, wob_ref, wo_ref, y_ref):
    br_a = jnp.dot(ya_ref[...], woa_ref[...], preferred_element_type=_F32)
    br_b = jnp.dot(yb_ref[...], wob_ref[...], preferred_element_type=_F32)
    mix = (jax.nn.sigmoid(ga_ref[...].astype(_F32)) * br_a
           + jax.nn.sigmoid(gb_ref[...].astype(_F32)) * br_b)
    y_ref[...] = x_ref[...] + jnp.dot(mix.astype(_BF16), wo_ref[...], preferred_element_type=_F32)


def _merge(x2d, y_a, y_b, pb, w_oa, w_ob, w_o, *, tm):
    m = x2d.shape[0]

    def rows(width, col=0):
        return pl.BlockSpec((tm, width), lambda i: (i, col))

    def whole(shape):
        return pl.BlockSpec(shape, lambda i: (0, 0), pipeline_mode=pl.Buffered(1))

    return pl.pallas_call(
        _merge_kernel,
        grid=(m // tm,),
        in_specs=[rows(D_MODEL), rows(D_A), rows(D_B),
                  rows(D_MODEL, P_GA * SECTION // D_MODEL), rows(D_MODEL, P_GB * SECTION // D_MODEL),
                  whole(w_oa.shape), whole(w_ob.shape), whole(w_o.shape)],
        out_specs=rows(D_MODEL),
        out_shape=jax.ShapeDtypeStruct((m, D_MODEL), _F32),
        compiler_params=pltpu.CompilerParams(
            dimension_semantics=("arbitrary",), vmem_limit_bytes=VMEM_LIMIT),
        name="merge",
    )(x2d, y_a, y_b, pb, pb, w_oa, w_ob, w_o)


def kernel(x_prompt, x_sample, cache_k, cache_v, page_table, norm_w, w_in, v_norm_w, q_norm_w,
           k_norm_w, w_spatial, b_spatial, w_out_a, w_out_b, w_out):
    depth = norm_w.shape[0]
    assert depth == 1, "single-layer trunk"
    n_bp, seq, _ = x_prompt.shape
    assert n_bp == 1 and seq % MOBA_BLOCK == 0
    n_bat, n_new, _ = x_sample.shape
    n_pages = page_table.shape[1]
    past = n_pages * PAGE_SIZE
    assert past % MOBA_BLOCK == 0 and n_pages // PAGES_PER_BLOCK >= MOBA_TOPK and n_new <= CHUNK
    layer = 0

    slopes = jnp.asarray(np.array([2.0 ** (-8.0 * (h + 1) / N_HEADS) for h in range(N_HEADS)], np.float32))
    ones = jnp.ones((SECTION,), _F32)
    epi_w = jnp.concatenate([
        ones, v_norm_w[layer], ones, jnp.tile(q_norm_w[layer], N_HEADS), jnp.tile(k_norm_w[layer], N_HEADS),
        ones, ones, ones, ones, ones, ones]).reshape(1, D_IN)
    nw = norm_w[layer].reshape(1, D_MODEL)
    w_in_b = w_in[layer].astype(_BF16)
    w_oa_b = w_out_a[layer].astype(_BF16)
    w_ob_b = w_out_b[layer].astype(_BF16)
    w_o_b = w_out[layer].astype(_BF16)

    xp = x_prompt.reshape(seq, D_MODEL)
    tm_p, tn_p, sub = 1024, 512, 4
    pages = page_table.reshape(-1)
    steps_f, steps_b = _proj_steps(seq, tm_p, tn_p)
    moba_pages = (_moba_iterations(seq, sub) * MOBA_PAGES_PER_ITER
                  + N_HEADS * (seq // (sub * MOBA_BLOCK)) * MOBA_PAGES_PER_STEP)
    per_f, per_b = _plan_page_stream(pages.shape[0], steps_f, steps_b, moba_pages)
    pf, pb, means = _project_both(xp, nw, w_in_b, epi_w, tm=tm_p, tn=tn_p,
                                  stream=(pages, cache_k[layer], per_f, per_b))
    y_a = _gmlp(pf, pb, w_spatial[layer], b_spatial[layer].T, tm=512)
    y_b, km_loop, km_steps = _moba(pf, pb, slopes, sub=sub,
                                   stream=(pages, cache_k[layer], per_f * steps_f + per_b * steps_b,
                                           MOBA_PAGES_PER_ITER, MOBA_PAGES_PER_STEP))
    k_mean = jnp.concatenate(means + [km_loop, km_steps], axis=0)
    y_prompt = _merge(xp, y_a, y_b, pb, w_oa_b, w_ob_b, w_o_b, tm=256)

    def section(arr, c):
        return arr[:, c * SECTION:(c + 1) * SECTION]

    k, vb = section(pf, F_K), section(pf, F_VB)

    m_s = n_bat * n_new
    xs = x_sample.reshape(m_s, D_MODEL)
    pf_s, pb_s = _project_both(xs, nw, w_in_b, epi_w, tm=m_s, tn=512)
    v_s, q_s, k_s, vb_s = (section(pf_s, c) for c in (F_V, F_Q, F_K, F_VB))
    u_s, za_s, zb_s = (section(pb_s, c) for c in (P_U, P_ZA, P_ZB))
    w4 = w_spatial[layer][:, :n_new, :n_new]
    wrow = jnp.repeat(w4.transpose(1, 2, 0), GROUP_DIM, axis=-1).reshape(n_new * n_new, D_A)
    brow = jnp.repeat(b_spatial[layer][:, :n_new].T, GROUP_DIM, axis=-1)
    three = (n_bat, n_new, D_A)
    ya_s = _gmlp_dec(u_s.reshape(three), v_s.reshape(three), za_s.reshape(three), wrow, brow)
    q3 = q_s.reshape(n_bat, n_new, D_B)
    n_full = n_pages // PAGES_PER_BLOCK
    idx_pad = _gate_topk(q3, k_mean.reshape(n_bat, n_full, N_HEADS, HEAD_DIM))
    idx = idx_pad[:, :, :MOBA_TOPK].reshape(n_bat, N_HEADS * n_new * MOBA_TOPK)
    yb_s = _moba_dec(page_table, idx, slopes, q3, k_s.reshape(n_bat, n_new, D_B),
                     vb_s.reshape(n_bat, n_new, D_B), zb_s.reshape(n_bat, n_new, D_B),
                     cache_k[layer], cache_v[layer], past=past)
    y_sample = _merge(xs, ya_s.reshape(m_s, D_A), yb_s.reshape(m_s, D_B), pb_s,
                      w_oa_b, w_ob_b, w_o_b, tm=m_s)

    return (y_prompt.reshape(x_prompt.shape),
            y_sample.reshape(x_sample.shape),
            k.reshape(depth, n_bp, seq, N_HEADS, HEAD_DIM),
            vb.reshape(depth, n_bp, seq, N_HEADS, HEAD_DIM),
            k_s.reshape(depth, n_bat, n_new, N_HEADS, HEAD_DIM),
            vb_s.reshape(depth, n_bat, n_new, N_HEADS, HEAD_DIM),
            v_s.reshape(depth, n_bat, n_new, D_A))
```
